```python
import math, functools
import jax, jax.numpy as jnp
from jax import lax
import numpy as np

D_MODEL = 1024
BATCH = 2
SEQ = 8192
DEPTH = 4
DEC_BATCH = 128
DEC_SEQ = 8
PAST_LEN = 2048
PAGE_SIZE = 128

HEAD_DIM = 64
N_HEADS_NSA = D_MODEL // 128
N_KV_NSA = 2
GROUP_SIZE = N_HEADS_NSA // N_KV_NSA
N_HEADS_SB = D_MODEL // 128
NSA_WIDTH = N_HEADS_NSA * HEAD_DIM
SB_WIDTH = N_HEADS_SB * HEAD_DIM
CMP_BLOCK = 32
SLC_BLOCK = 64
N_SEL = 16
WINDOW = 512
Q_BLOCK = 128
ROPE_THETA = 10000.0
N_GROUPS = 4
EXPERTS_PER_GROUP = 4
N_EXPERTS = N_GROUPS * EXPERTS_PER_GROUP
TOP_K = 2
D_EXPERT = 256
EPS = 1e-6
FORCE_SCORE = 1e4
NEG_INF = -1e30

COL_Q = NSA_WIDTH
COL_KV = 6 * N_KV_NSA * HEAD_DIM
COL_GN = 3 * N_HEADS_NSA
COL_SB = 3 * SB_WIDTH
COL_MERGE = 2 * D_MODEL
IN_COLS = COL_Q + COL_KV + COL_GN + COL_SB + COL_MERGE

kernel_name = 'hybrid_nsa_stickbreak_hmoe_decode_step'


def rmsnorm(x, g):
    x32 = x.astype(jnp.float32)
    y = x32 * lax.rsqrt(jnp.mean(x32 * x32, axis=-1, keepdims=True) + EPS)
    return y.astype(x.dtype) * g


def rope(x, pos):
    inv_freq = ROPE_THETA ** (-jnp.arange(0, HEAD_DIM, 2, dtype=jnp.float32) / HEAD_DIM)
    ang = pos.astype(jnp.float32)[:, None] * inv_freq[None, :]
    cos = jnp.cos(ang).astype(x.dtype)[:, None, :]
    sin = jnp.sin(ang).astype(x.dtype)[:, None, :]
    x1, x2 = x[..., :HEAD_DIM // 2], x[..., HEAD_DIM // 2:]
    return jnp.concatenate([x1 * cos - x2 * sin, x2 * cos + x1 * sin], axis=-1)


def ada_modulation(c, w_ada_l, b_ada_l):
    mod = jax.nn.silu(c) @ w_ada_l + b_ada_l
    return jnp.split(mod[:, None, :], 6, axis=-1)


def mixer_inputs(h, pos, w_in_l, qkn_l):
    B, T, _ = h.shape
    p = jnp.einsum('btd,dc->btc', h, w_in_l)
    o1 = COL_Q
    o2 = o1 + COL_KV
    o3 = o2 + COL_GN
    o4 = o3 + COL_SB
    q = p[..., :o1].reshape(B, T, N_HEADS_NSA, HEAD_DIM)
    kv = p[..., o1:o2].reshape(B, T, 6, N_KV_NSA, HEAD_DIM)
    gn = jax.nn.sigmoid(p[..., o2:o3]).reshape(B, T, N_HEADS_NSA, 3)
    sb = p[..., o3:o4].reshape(B, T, 3, N_HEADS_SB, HEAD_DIM)
    gm = jax.nn.sigmoid(p[..., o4:]).reshape(B, T, 2, D_MODEL)
    q = rope(rmsnorm(q, qkn_l[0]), pos)
    k_c = rope(rmsnorm(kv[:, :, 0], qkn_l[1]), pos)
    k_s = rope(rmsnorm(kv[:, :, 2], qkn_l[2]), pos)
    k_w = rope(rmsnorm(kv[:, :, 4], qkn_l[3]), pos)
    nsa_rows = jnp.stack([k_c, kv[:, :, 1], k_s, kv[:, :, 3]], axis=2)
    win_rows = jnp.stack([k_w, kv[:, :, 5]], axis=2)
    return q, gn, nsa_rows, win_rows, sb[:, :, 0], sb[:, :, 1:], gm


def nsa_prepare(nsa_keys, w_cmp_l):
    B, T = nsa_keys.shape[:2]
    cb = nsa_keys[:, :, :2].reshape(B, T // CMP_BLOCK, CMP_BLOCK, 2, N_KV_NSA, HEAD_DIM)
    kc = jnp.einsum('bnlgd,ld->bngd', cb[:, :, :, 0], w_cmp_l[0])
    vc = jnp.einsum('bnlgd,ld->bngd', cb[:, :, :, 1], w_cmp_l[1])
    sb_ = nsa_keys[:, :, 2:].reshape(B, T // SLC_BLOCK, SLC_BLOCK, 2, N_KV_NSA, HEAD_DIM)
    ks_b = jnp.transpose(sb_[:, :, :, 0], (0, 3, 1, 2, 4))
    vs_b = jnp.transpose(sb_[:, :, :, 1], (0, 3, 1, 2, 4))
    return kc, vc, ks_b, vs_b


def gather_blocks(blocks, ids):
    return jax.vmap(jax.vmap(lambda b_, i_: b_[i_]))(blocks, ids)


def nsa_core(q, gn, pos_q, kc, vc, ks_b, vs_b, kw, vw, pos_kw):
    B, Qn = q.shape[:2]
    scale = HEAD_DIM ** -0.5
    qg = q.reshape(B, Qn, N_KV_NSA, GROUP_SIZE, HEAD_DIM)
    n_cmp = kc.shape[1]
    s_c = jnp.einsum('bqgrd,bngd->bqgrn', qg, kc, preferred_element_type=jnp.float32) * scale
    m_c = (((jnp.arange(n_cmp) + 1) * CMP_BLOCK - 1)[None, :] <= pos_q[:, None])[None, :, None, None, :]
    p_c = jnp.where(m_c, jax.nn.softmax(jnp.where(m_c, s_c, NEG_INF), axis=-1), 0.0)
    o_c = jnp.einsum('bqgrn,bngd->bqgrd', p_c.astype(vc.dtype), vc)
    n_slc = ks_b.shape[2]
    imp = p_c.sum(axis=3).reshape(B, Qn, N_KV_NSA, n_slc, SLC_BLOCK // CMP_BLOCK).sum(axis=-1)
    cur = pos_q // SLC_BLOCK
    j = jnp.arange(n_slc)[None, :]
    forced = (j == 0) | (j == cur[:, None]) | (j == cur[:, None] - 1)
    valid = j <= cur[:, None]
    score = jnp.where(forced[None, :, None, :], FORCE_SCORE,
                      jnp.where(valid[None, :, None, :], imp, -1.0))
    n_sel = min(N_SEL, n_slc)
    _, idx = lax.top_k(score, n_sel)
    idx_g = jnp.transpose(idx, (0, 2, 1, 3)).reshape(B, N_KV_NSA, Qn * n_sel)
    k_sel = gather_blocks(ks_b, idx_g).reshape(B, N_KV_NSA, Qn, n_sel, SLC_BLOCK, HEAD_DIM)
    v_sel = gather_blocks(vs_b, idx_g).reshape(B, N_KV_NSA, Qn, n_sel, SLC_BLOCK, HEAD_DIM)
    pos_ks = idx[..., None] * SLC_BLOCK + jnp.arange(SLC_BLOCK)
    m_s = (pos_ks <= pos_q[None, :, None, None, None])[:, :, :, None]
    s_s = jnp.einsum('bqgrd,bgqnld->bqgrnl', qg, k_sel, preferred_element_type=jnp.float32) * scale
    s_s = jnp.where(m_s, s_s, NEG_INF).reshape(B, Qn, N_KV_NSA, GROUP_SIZE, n_sel * SLC_BLOCK)
    p_s = jax.nn.softmax(s_s, axis=-1).reshape(B, Qn, N_KV_NSA, GROUP_SIZE, n_sel, SLC_BLOCK)
    o_s = jnp.einsum('bqgrnl,bgqnld->bqgrd', p_s.astype(v_sel.dtype), v_sel)
    s_w = jnp.einsum('bqgrd,bkgd->bqgrk', qg, kw, preferred_element_type=jnp.float32) * scale
    dist = pos_q[:, None] - pos_kw[None, :]
    m_w = ((dist >= 0) & (dist < WINDOW) & (pos_kw[None, :] >= 0))[None, :, None, None, :]
    p_w = jax.nn.softmax(jnp.where(m_w, s_w, NEG_INF), axis=-1)
    o_w = jnp.einsum('bqgrk,bkgd->bqgrd', p_w.astype(vw.dtype), vw)
    g = gn.reshape(B, Qn, N_KV_NSA, GROUP_SIZE, 3)
    o = g[..., 0:1] * o_c + g[..., 1:2] * o_s + g[..., 2:3] * o_w
    return o.reshape(B, Qn, N_HEADS_NSA, HEAD_DIM)


def stick_breaking(q, k, v, pos_q, pos_k):
    z = jnp.einsum('bqhd,bkhd->bhqk', q, k, preferred_element_type=jnp.float32) * (HEAD_DIM ** -0.5)
    causal = (pos_k[None, :] < pos_q[:, None])[None, None]
    log_keep = jnp.where(causal, jax.nn.log_sigmoid(-z), 0.0)
    later = lax.cumsum(log_keep, axis=3, reverse=True) - log_keep
    a = jnp.where(causal, jnp.exp(jax.nn.log_sigmoid(z) + later), 0.0)
    return jnp.einsum('bhqk,bkhd->bqhd', a.astype(v.dtype), v)


def prompt_attend(q, gn, nsa_rows, win_rows, sb_q, sb_rows, w_cmp_l):
    B, S = q.shape[:2]
    kc, vc, ks_b, vs_b = nsa_prepare(nsa_rows, w_cmp_l)
    band_src = jnp.pad(win_rows, ((0, 0), (WINDOW, 0), (0, 0), (0, 0), (0, 0)))
    k_sb, v_sb = sb_rows[:, :, 0], sb_rows[:, :, 1]
    pos_k = jnp.arange(S)
    band_len = WINDOW + Q_BLOCK

    def one_block(i):
        q0 = i * Q_BLOCK
        pos_q = q0 + jnp.arange(Q_BLOCK)
        qb = lax.dynamic_slice_in_dim(q, q0, Q_BLOCK, axis=1)
        gb = lax.dynamic_slice_in_dim(gn, q0, Q_BLOCK, axis=1)
        band = lax.dynamic_slice_in_dim(band_src, q0, band_len, axis=1)
        pos_kw = q0 - WINDOW + jnp.arange(band_len)
        o_n = nsa_core(qb, gb, pos_q, kc, vc, ks_b, vs_b, band[:, :, 0], band[:, :, 1], pos_kw)
        sbq = lax.dynamic_slice_in_dim(sb_q, q0, Q_BLOCK, axis=1)
        o_s = stick_breaking(sbq, k_sb, v_sb, pos_q, pos_k)
        return o_n, o_s

    o_n, o_s = lax.map(one_block, jnp.arange(S // Q_BLOCK))
    o_n = jnp.moveaxis(o_n, 0, 1).reshape(B, S, N_HEADS_NSA, HEAD_DIM)
    o_s = jnp.moveaxis(o_s, 0, 1).reshape(B, S, N_HEADS_SB, HEAD_DIM)
    win_state = win_rows[:, S - min(WINDOW, S):]
    return o_n, o_s, win_state


def sample_attend(q, gn, nsa_rows, win_rows, sb_q, sb_rows, past_nsa, past_sb, past_win, w_cmp_l):
    B, t_new = q.shape[:2]
    t_past = past_nsa.shape[1]
    t_all = t_past + t_new
    t_pad = -(-t_all // SLC_BLOCK) * SLC_BLOCK
    pos_q = t_past + jnp.arange(t_new)
    all_nsa = jnp.concatenate([past_nsa, nsa_rows], axis=1)
    all_nsa = jnp.pad(all_nsa, ((0, 0), (0, t_pad - t_all), (0, 0), (0, 0), (0, 0)))
    kc, vc, ks_b, vs_b = nsa_prepare(all_nsa, w_cmp_l)
    band = jnp.concatenate([past_win, win_rows], axis=1)
    pos_kw = t_past - past_win.shape[1] + jnp.arange(band.shape[1])
    o_n = nsa_core(q, gn, pos_q, kc, vc, ks_b, vs_b, band[:, :, 0], band[:, :, 1], pos_kw)
    all_sb = jnp.concatenate([past_sb, sb_rows], axis=1)
    o_s = stick_breaking(sb_q, all_sb[:, :, 0], all_sb[:, :, 1], pos_q, jnp.arange(t_all))
    win_state = band[:, t_new:]
    return o_n, o_s, win_state


def hier_moe(h, w_rg, b_rg, w_re, b_re, w_e1, w_e3, w_e2):
    B, T, D = h.shape
    ht = h.reshape(B * T, D)
    lg = (ht @ w_rg).astype(jnp.float32) + b_rg.astype(jnp.float32)
    grp = jnp.argmax(lg, axis=-1)
    p_grp = jnp.take_along_axis(jax.nn.softmax(lg, axis=-1), grp[:, None], axis=-1)
    le = ((ht @ w_re).astype(jnp.float32) + b_re.astype(jnp.float32)).reshape(-1, N_GROUPS, EXPERTS_PER_GROUP)
    le_in = jnp.take_along_axis(le, grp[:, None, None], axis=1)[:, 0]
    top_val, top_idx = lax.top_k(le_in, TOP_K)
    w_top = p_grp * jax.nn.softmax(top_val, axis=-1)
    eid = grp[:, None] * EXPERTS_PER_GROUP + top_idx
    gate = jnp.sum(jax.nn.one_hot(eid, N_EXPERTS, dtype=jnp.float32) * w_top[..., None], axis=1)
    hid = jax.nn.silu(jnp.einsum('nd,edf->nef', ht, w_e1)) * jnp.einsum('nd,edf->nef', ht, w_e3)
    hid = hid * gate.astype(hid.dtype)[..., None]
    y = jnp.einsum('nef,efd->nd', hid, w_e2)
    return y.reshape(B, T, D)


def layer(x, c, pos, attend, w_ada_l, b_ada_l, g_norm_l, w_in_l, qkn_l, w_br_a_l, w_br_b_l, w_out_l,
          w_rg_l, b_rg_l, w_re_l, b_re_l, w_e1_l, w_e3_l, w_e2_l):
    B, T = x.shape[:2]
    sh1, sc1, gt1, sh2, sc2, gt2 = ada_modulation(c, w_ada_l, b_ada_l)
    h = rmsnorm(x, g_norm_l[0]) * (1 + sc1) + sh1
    q, gn, nsa_rows, win_rows, sb_q, sb_rows, gm = mixer_inputs(h, pos, w_in_l, qkn_l)
    o_n, o_s, win_state = attend(q, gn, nsa_rows, win_rows, sb_q, sb_rows)
    y_a = o_n.reshape(B, T, NSA_WIDTH) @ w_br_a_l
    y_b = o_s.reshape(B, T, SB_WIDTH) @ w_br_b_l
    x = x + gt1 * ((gm[:, :, 0] * y_a + gm[:, :, 1] * y_b) @ w_out_l)
    h2 = rmsnorm(x, g_norm_l[1]) * (1 + sc2) + sh2
    x = x + gt2 * hier_moe(h2, w_rg_l, b_rg_l, w_re_l, b_re_l, w_e1_l, w_e3_l, w_e2_l)
    return x, nsa_rows, sb_rows, win_state


def setup_inputs(seed: int = 0) -> dict:
    key = jax.random.key(seed)
    ks = iter(jax.random.split(key, 32))

    def nrm(shape, scale=1.0):
        return jax.random.normal(next(ks), shape, jnp.float32) * scale

    n_pages = PAST_LEN // PAGE_SIZE
    n_used = DEC_BATCH * n_pages
    n_phys = n_used + max(1, n_used // 4)
    w_buf = min(WINDOW, PAST_LEN)
    return {
        'x_prompt': nrm((BATCH, SEQ, D_MODEL)),
        'x_sample': nrm((DEC_BATCH, DEC_SEQ, D_MODEL)),
        'cache_nsa': nrm((DEPTH, n_phys, PAGE_SIZE, 4, N_KV_NSA, HEAD_DIM)),
        'cache_sb': nrm((DEPTH, n_phys, PAGE_SIZE, 2, N_HEADS_SB, HEAD_DIM)),
        'state_win': nrm((DEPTH, DEC_BATCH, w_buf, 2, N_KV_NSA, HEAD_DIM)),
        'page_table': jax.random.permutation(next(ks), n_phys)[:n_used].reshape(DEC_BATCH, n_pages).astype(jnp.int32),
        'c_prompt': nrm((BATCH, D_MODEL)),
        'c_sample': nrm((DEC_BATCH, D_MODEL)),
        'w_ada': nrm((DEPTH, D_MODEL, 6 * D_MODEL), 0.5 * D_MODEL ** -0.5),
        'b_ada': nrm((DEPTH, 6 * D_MODEL), 0.02),
        'g_norm': 1.0 + nrm((DEPTH, 2, D_MODEL), 0.1),
        'w_in': nrm((DEPTH, D_MODEL, IN_COLS), D_MODEL ** -0.5),
        'w_qk_norm': 1.0 + nrm((DEPTH, 4, HEAD_DIM), 0.1),
        'w_cmp': (1.0 + nrm((DEPTH, 2, CMP_BLOCK, HEAD_DIM), 0.2)) / CMP_BLOCK,
        'w_br_a': nrm((DEPTH, NSA_WIDTH, D_MODEL), NSA_WIDTH ** -0.5),
        'w_br_b': nrm((DEPTH, SB_WIDTH, D_MODEL), SB_WIDTH ** -0.5),
        'w_out': nrm((DEPTH, D_MODEL, D_MODEL), D_MODEL ** -0.5),
        'w_route_group': nrm((DEPTH, D_MODEL, N_GROUPS), D_MODEL ** -0.5),
        'b_route_group': nrm((DEPTH, N_GROUPS), 0.01),
        'w_route_expert': nrm((DEPTH, D_MODEL, N_EXPERTS), D_MODEL ** -0.5),
        'b_route_expert': nrm((DEPTH, N_EXPERTS), 0.01),
        'w_expert_gate': nrm((DEPTH, N_EXPERTS, D_MODEL, D_EXPERT), D_MODEL ** -0.5),
        'w_expert_up': nrm((DEPTH, N_EXPERTS, D_MODEL, D_EXPERT), D_MODEL ** -0.5),
        'w_expert_down': nrm((DEPTH, N_EXPERTS, D_EXPERT, D_MODEL), D_EXPERT ** -0.5),
    }


def reference(x_prompt, x_sample, cache_nsa, cache_sb, state_win, page_table, c_prompt, c_sample,
              w_ada, b_ada, g_norm, w_in, w_qk_norm, w_cmp, w_br_a, w_br_b, w_out,
              w_route_group, b_route_group, w_route_expert, b_route_expert,
              w_expert_gate, w_expert_up, w_expert_down):
    pos_p = jnp.arange(SEQ)
    pos_s = PAST_LEN + jnp.arange(DEC_SEQ)
    xp, xs = x_prompt, x_sample
    nsa_p_l, sb_p_l, win_p_l = [], [], []
    nsa_s_l, sb_s_l, win_s_l = [], [], []
    for l in range(DEPTH):
        lw = (w_ada[l], b_ada[l], g_norm[l], w_in[l], w_qk_norm[l], w_br_a[l], w_br_b[l], w_out[l],
              w_route_group[l], b_route_group[l], w_route_expert[l], b_route_expert[l],
              w_expert_gate[l], w_expert_up[l], w_expert_down[l])
        attend_p = functools.partial(prompt_attend, w_cmp_l=w_cmp[l])
        xp, nsa_p, sb_p, win_p = layer(xp, c_prompt, pos_p, attend_p, *lw)
        past_nsa = cache_nsa[l][page_table].reshape(DEC_BATCH, -1, 4, N_KV_NSA, HEAD_DIM)
        past_sb = cache_sb[l][page_table].reshape(DEC_BATCH, -1, 2, N_HEADS_SB, HEAD_DIM)
        attend_s = functools.partial(sample_attend, past_nsa=past_nsa, past_sb=past_sb,
                                     past_win=state_win[l], w_cmp_l=w_cmp[l])
        xs, nsa_s, sb_s, win_s = layer(xs, c_sample, pos_s, attend_s, *lw)
        nsa_p_l.append(nsa_p)
        sb_p_l.append(sb_p)
        win_p_l.append(win_p)
        nsa_s_l.append(nsa_s)
        sb_s_l.append(sb_s)
        win_s_l.append(win_s)
    new_nsa_prompt = jnp.stack(nsa_p_l)
    new_sb_prompt = jnp.stack(sb_p_l)
    new_win_prompt = jnp.stack(win_p_l)
    new_nsa_sample = jnp.stack(nsa_s_l)
    new_sb_sample = jnp.stack(sb_s_l)
    new_win_sample = jnp.stack(win_s_l)
    return (xp, xs, new_nsa_prompt, new_sb_prompt, new_win_prompt, new_nsa_sample, new_sb_sample, new_win_sample)
```

```python
import functools

import numpy as np
import jax
import jax.numpy as jnp
from jax import lax
from jax.experimental import pallas as pl
from jax.experimental.pallas import tpu as pltpu

F32 = jnp.float32
BF16 = jnp.bfloat16

D_MODEL = 1024
HEAD_DIM = 64
N_HEADS_NSA = 8
N_KV_NSA = 2
GROUP_SIZE = 4
N_HEADS_SB = 8
NSA_WIDTH = 512
SB_WIDTH = 512
CMP_BLOCK = 32
SLC_BLOCK = 64
N_SEL = 16
WINDOW = 512
ROPE_THETA = 10000.0
N_GROUPS = 4
EXPERTS_PER_GROUP = 4
N_EXPERTS = 16
D_EXPERT = 256
EPS = 1e-6
FORCE_SCORE = 1e4
NEG_INF = -1e30
SCALE = HEAD_DIM ** -0.5

LANES = 128
C_Q = 0
C_KV = 512
C_GN = 1280
C_SB = 1408
C_GM = 2944
C_END = 4992
VMEM_LIMIT = 56 * 1024 * 1024

_QPERM = np.concatenate([np.arange(h * 64, (h + 1) * 64) for s in range(4) for h in (s, 4 + s)])


def _nt(a, b):
    return lax.dot_general(a, b, (((1,), (1,)), ((), ())), preferred_element_type=F32)


def _mm(a, b):
    return jnp.dot(a, b, preferred_element_type=F32)


def _split_hi_lo(x):
    hi = x.astype(BF16)
    lo = (x - hi.astype(F32)).astype(BF16)
    return hi, lo


def _cparams(sem):
    return pltpu.CompilerParams(dimension_semantics=sem, vmem_limit_bytes=VMEM_LIMIT)


def _ada_body(c_ref, w_ref, b_ref, o_ref):
    c = c_ref[...]
    a = (c * jax.nn.sigmoid(c)).astype(BF16)
    o_ref[...] = _mm(a, w_ref[...].astype(BF16)) + b_ref[...]


def _ada_call(c_all, w_ada, b_ada):
    depth = w_ada.shape[0]
    n = c_all.shape[0]
    tn = 1536
    return pl.pallas_call(
        _ada_body,
        out_shape=jax.ShapeDtypeStruct((depth, n, 6 * D_MODEL), F32),
        grid=(depth, 6 * D_MODEL // tn),
        in_specs=[pl.BlockSpec((n, D_MODEL), lambda l, j: (0, 0)),
                  pl.BlockSpec((None, D_MODEL, tn), lambda l, j: (l, 0, j)),
                  pl.BlockSpec((None, 1, tn), lambda l, j: (l, 0, j))],
        out_specs=pl.BlockSpec((None, n, tn), lambda l, j: (l, 0, j)),
        compiler_params=_cparams(("parallel", "parallel")),
        name="ada_mod",
    )(c_all, w_ada, b_ada.reshape(depth, 1, 6 * D_MODEL))


def _inproj_body(x_ref, sc_ref, sh_ref, g_ref, w_ref, qkn_ref, cos_ref, sin_ref, bd_ref, wc_ref,
                 *outs, with_cmp):
    if with_cmp:
        (q_ref, nsa_ref, nsab_ref, win_ref, winb_ref, gn_ref, sbq_ref, sbkv_ref, sbkvb_ref,
         gm_ref, kc_ref, vc_ref) = outs
    else:
        (q_ref, nsa_ref, nsab_ref, win_ref, winb_ref, gn_ref, sbq_ref, sbkv_ref, sbkvb_ref,
         gm_ref) = outs
    x = x_ref[...]
    tm = x.shape[0]
    ms = jnp.mean(x * x, axis=-1, keepdims=True)
    h = (x * lax.rsqrt(ms + EPS)) * g_ref[...]
    h = h * (1.0 + sc_ref[...]) + sh_ref[...]
    hb = h.astype(BF16)
    cos = cos_ref[...]
    sin = sin_ref[...]
    bd = bd_ref[...]
    lane = lax.broadcasted_iota(jnp.int32, (1, LANES), 1)
    first_half = (lane & (HEAD_DIM - 1)) < HEAD_DIM // 2

    def seg(lo, hi):
        return _mm(hb, w_ref[:, lo:hi])

    def norm_rope(xs, gq):
        hi, lo = _split_hi_lo(xs * xs)
        ssum = _mm(jnp.concatenate([hi, lo], axis=1), bd)
        y = xs * lax.rsqrt(ssum * (1.0 / HEAD_DIM) + EPS) * gq
        sw = jnp.where(first_half, pltpu.roll(y, 96, 1), pltpu.roll(y, 32, 1))
        return y * cos + sw * sin

    pq = seg(C_Q, C_KV)
    for s in range(4):
        qs = norm_rope(pq[:, s * LANES:(s + 1) * LANES], qkn_ref[0:1, :]) * SCALE
        q_ref[:, s * LANES:(s + 1) * LANES] = qs.astype(q_ref.dtype)

    pkv = seg(C_KV, C_GN)
    kc = norm_rope(pkv[:, 0:128], qkn_ref[1:2, :])
    vc = pkv[:, 128:256]
    ks = norm_rope(pkv[:, 256:384], qkn_ref[2:3, :])
    vs = pkv[:, 384:512]
    kw = norm_rope(pkv[:, 512:640], qkn_ref[3:4, :])
    vw = pkv[:, 640:768]
    for j, v in enumerate((kc, vc, ks, vs)):
        nsa_ref[:, j * LANES:(j + 1) * LANES] = v
        nsab_ref[:, j * LANES:(j + 1) * LANES] = v.astype(BF16)
    for j, v in enumerate((kw, vw)):
        win_ref[:, j * LANES:(j + 1) * LANES] = v
        winb_ref[:, j * LANES:(j + 1) * LANES] = v.astype(BF16)
    if with_cmp:
        nb = tm // CMP_BLOCK
        kc_ref[...] = jnp.sum(kc.reshape(nb, CMP_BLOCK, LANES) * wc_ref[0][None], axis=1)
        vc_ref[...] = jnp.sum(vc.reshape(nb, CMP_BLOCK, LANES) * wc_ref[1][None], axis=1)

    gn_ref[...] = jax.nn.sigmoid(seg(C_GN, C_SB))

    psb = seg(C_SB, C_GM)
    sbq_ref[...] = (psb[:, 0:SB_WIDTH] * SCALE).astype(sbq_ref.dtype)
    sbkv_ref[...] = psb[:, SB_WIDTH:]
    sbkvb_ref[...] = psb[:, SB_WIDTH:].astype(BF16)

    gm_ref[...] = jax.nn.sigmoid(seg(C_GM, C_END))


def _inproj_call(x, sc, sh, g_norm, w_in_p, qkn_t, cos_t, sin_t, bd, wc_t, layer, *, tm, with_cmp,
                 qdtype):
    bsz, t, _ = x.shape
    mrows = sc.shape[1]
    mblk = 1 if mrows == 1 else tm
    mod_map = (lambda b, i: (b, 0, 0)) if mrows == 1 else (lambda b, i: (b, i, 0))
    tok = lambda w: pl.BlockSpec((None, tm, w), lambda b, i: (b, i, 0))
    in_specs = [
        tok(D_MODEL),
        pl.BlockSpec((None, mblk, D_MODEL), mod_map),
        pl.BlockSpec((None, mblk, D_MODEL), mod_map),
        pl.BlockSpec((None, None, 1, D_MODEL), lambda b, i: (layer, 0, 0, 0)),
        pl.BlockSpec((None, D_MODEL, C_END), lambda b, i: (layer, 0, 0)),
        pl.BlockSpec((None, 4, LANES), lambda b, i: (layer, 0, 0)),
        pl.BlockSpec((tm, LANES), lambda b, i: (i, 0)),
        pl.BlockSpec((tm, LANES), lambda b, i: (i, 0)),
        pl.BlockSpec((2 * LANES, LANES), lambda b, i: (0, 0)),
        pl.BlockSpec((None, 2, CMP_BLOCK, LANES), lambda b, i: (layer, 0, 0, 0)),
    ]
    sds = lambda w, dt: jax.ShapeDtypeStruct((bsz, t, w), dt)
    out_shape = [sds(512, qdtype), sds(512, F32), sds(512, BF16), sds(256, F32), sds(256, BF16),
                 sds(128, F32), sds(512, qdtype), sds(1024, F32), sds(1024, BF16), sds(2048, F32)]
    out_specs = [tok(512), tok(512), tok(512), tok(256), tok(256), tok(128), tok(512), tok(1024),
                 tok(1024), tok(2048)]
    if with_cmp:
        nb = tm // CMP_BLOCK
        out_shape += [jax.ShapeDtypeStruct((bsz, t // CMP_BLOCK, LANES), F32)] * 2
        out_specs += [pl.BlockSpec((None, nb, LANES), lambda b, i: (b, i, 0))] * 2
    return pl.pallas_call(
        functools.partial(_inproj_body, with_cmp=with_cmp),
        out_shape=out_shape,
        grid=(bsz, t // tm),
        in_specs=in_specs,
        out_specs=out_specs,
        compiler_params=_cparams(("parallel", "parallel")),
        name="inproj",
    )(x, sc, sh, g_norm, w_in_p, qkn_t, cos_t, sin_t, bd, wc_t)


TQ = 128
TKS = 512


def _masked_softmax(s, mask):
    sm = jnp.where(mask, s, NEG_INF)
    mx = jnp.max(sm, axis=-1, keepdims=True)
    e = jnp.exp(sm - mx)
    den = jnp.sum(e, axis=-1, keepdims=True)
    return jnp.where(mask, e / den, 0.0)


def _top_sel_rows(score, n_pick):
    nbk = score.shape[0]
    jrow = lax.broadcasted_iota(jnp.int32, score.shape, 0).astype(F32)
    sel = jnp.zeros(score.shape, F32)
    for _ in range(n_pick):
        m = jnp.max(score, axis=0, keepdims=True)
        idx = jnp.min(jnp.where(score == m, jrow, float(nbk)), axis=0, keepdims=True)
        pick = jrow == idx
        sel = jnp.where(pick, 1.0, sel)
        score = jnp.where(pick, -jnp.inf, score)
    return sel


def _nsa_prompt_body(q_ref, gn_ref, kc_ref, vc_ref, ks_ref, vs_ref, kw_ref, vw_ref, o_ref):
    i = pl.program_id(1)
    q0 = i * TQ
    t_all = ks_ref.shape[0]
    ncmp = kc_ref.shape[0]
    nslc = ncmp // 2
    lane = lax.broadcasted_iota(jnp.int32, (1, LANES), 1)
    qall = q_ref[...]
    gates = gn_ref[...]
    posq = q0 + lax.broadcasted_iota(jnp.int32, (TQ, 1), 0)
    posq4 = q0 + (lax.broadcasted_iota(jnp.int32, (GROUP_SIZE * TQ, 1), 0) & (TQ - 1))

    lc = lax.broadcasted_iota(jnp.int32, (1, ncmp), 1)
    blk = jnp.where(lc < nslc, 2 * lc, 2 * (lc - nslc) + 1)
    mc4 = ((blk + 1) * CMP_BLOCK - 1) <= posq4

    jrow = lax.broadcasted_iota(jnp.int32, (nslc, TQ), 0)
    cur = (q0 + lax.broadcasted_iota(jnp.int32, (nslc, TQ), 1)) >> 6
    forced = (jrow == 0) | (jrow == cur) | (jrow == cur - 1)
    valid = jrow <= cur

    wlen = WINDOW + TQ
    wstart = pl.multiple_of(jnp.maximum(q0 - WINDOW, 0), TQ)
    posw = wstart + lax.broadcasted_iota(jnp.int32, (1, wlen), 1)
    dist = posq4 - posw
    mw4 = (dist >= 0) & (dist < WINDOW)

    n_tiles = (q0 + TQ + TKS - 1) // TKS
    out = [jnp.zeros((TQ, LANES), F32) for _ in range(4)]
    for g in range(N_KV_NSA):
        lmask = (lane < HEAD_DIM) if g == 0 else (lane >= HEAD_DIM)
        zero = jnp.zeros((), qall.dtype)
        qg = jnp.concatenate(
            [jnp.where(lmask, qall[:, s * LANES:(s + 1) * LANES], zero) for s in range(4)], axis=0)

        pc = _masked_softmax(_nt(qg, kc_ref[...]), mc4)
        oc = _mm(pc.astype(BF16), vc_ref[...])
        imp = pc[0:TQ] + pc[TQ:2 * TQ] + pc[2 * TQ:3 * TQ] + pc[3 * TQ:4 * TQ]
        imp = imp[:, 0:nslc] + imp[:, nslc:ncmp]
        if nslc < LANES:
            imp = jnp.concatenate([imp, jnp.zeros((TQ, LANES - nslc), F32)], axis=1)
        imp_t = imp.T[0:nslc, :]
        score = jnp.where(forced, FORCE_SCORE, jnp.where(valid, imp_t, -1.0))
        sel_t = _top_sel_rows(score, min(N_SEL, nslc))
        if nslc < LANES:
            sel_t = jnp.concatenate([sel_t, jnp.zeros((LANES - nslc, TQ), F32)], axis=0)
        selq = sel_t.T.astype(BF16)

        def sel_tile(t, carry):
            m_old, l_old, acc = carry
            k0 = pl.multiple_of(t * TKS, TKS)
            s = _nt(qg, ks_ref[pl.ds(k0, TKS), :])
            kpos = k0 + lax.broadcasted_iota(jnp.int32, (LANES, TKS), 1)
            brow = lax.broadcasted_iota(jnp.int32, (LANES, TKS), 0)
            expand = jnp.where(brow == (kpos >> 6), 1.0, 0.0).astype(BF16)
            picked = _mm(selq, expand)
            allowed = (picked > 0.5) & (kpos[0:1, :] <= posq)
            bias = jnp.where(allowed, 0.0, NEG_INF)
            s = jnp.concatenate([s[r * TQ:(r + 1) * TQ] + bias for r in range(GROUP_SIZE)], axis=0)
            m_new = jnp.maximum(m_old, jnp.max(s, axis=-1, keepdims=True))
            alpha = jnp.exp(m_old - m_new)
            p = jnp.exp(s - m_new)
            l_new = alpha * l_old + jnp.sum(p, axis=-1, keepdims=True)
            acc = alpha * acc + _mm(p.astype(BF16), vs_ref[pl.ds(k0, TKS), :])
            return m_new, l_new, acc

        m0 = jnp.full((4 * TQ, 1), NEG_INF, F32)
        l0 = jnp.zeros((4 * TQ, 1), F32)
        a0 = jnp.zeros((4 * TQ, LANES), F32)
        _, l_s, acc_s = lax.fori_loop(0, n_tiles, sel_tile, (m0, l0, a0))
        osel = acc_s / l_s

        pw = _masked_softmax(_nt(qg, kw_ref[pl.ds(wstart, wlen), :]), mw4)
        ow = _mm(pw.astype(BF16), vw_ref[pl.ds(wstart, wlen), :])

        for s in range(4):
            hd = s + 4 * g
            r = slice(s * TQ, (s + 1) * TQ)
            comb = (gates[:, 3 * hd:3 * hd + 1] * oc[r] + gates[:, 3 * hd + 1:3 * hd + 2] * osel[r]
                    + gates[:, 3 * hd + 2:3 * hd + 3] * ow[r])
            out[s] = jnp.where(lmask, comb, out[s])
    for s in range(4):
        o_ref[:, s * LANES:(s + 1) * LANES] = out[s].astype(o_ref.dtype)


def _nsa_prompt_call(q, gn, kcp, vcp, nsab, winb):
    bsz, t, _ = q.shape
    ncmp = kcp.shape[1]
    full = lambda col: pl.BlockSpec((None, t, LANES), lambda b, i, c=col: (b, 0, c))
    return pl.pallas_call(
        _nsa_prompt_body,
        out_shape=jax.ShapeDtypeStruct((bsz, t, NSA_WIDTH), BF16),
        grid=(bsz, t // TQ),
        in_specs=[pl.BlockSpec((None, TQ, NSA_WIDTH), lambda b, i: (b, i, 0)),
                  pl.BlockSpec((None, TQ, LANES), lambda b, i: (b, i, 0)),
                  pl.BlockSpec((None, ncmp, LANES), lambda b, i: (b, 0, 0)),
                  pl.BlockSpec((None, ncmp, LANES), lambda b, i: (b, 0, 0)),
                  full(2), full(3), full(0), full(1)],
        out_specs=pl.BlockSpec((None, TQ, NSA_WIDTH), lambda b, i: (b, i, 0)),
        compiler_params=_cparams(("parallel", "parallel")),
        name="nsa_prompt",
    )(q, gn, kcp, vcp, nsab, nsab, winb, winb)


TKB = 256


def _softplus_parts(z):
    lg = jnp.log(1.0 + jnp.exp(-jnp.abs(z)))
    return jnp.maximum(z, 0.0) + lg, jnp.minimum(z, 0.0) - lg


def _suffix_matrix(n):
    r = lax.broadcasted_iota(jnp.int32, (n, n), 0)
    c = lax.broadcasted_iota(jnp.int32, (n, n), 1)
    return jnp.where(r > c, 1.0, 0.0).astype(BF16)


def _sb_prompt_body(q_ref, k_ref, v_ref, o_ref):
    i = pl.program_id(2)
    q0 = i * TQ
    lane = lax.broadcasted_iota(jnp.int32, (1, LANES), 1)
    q2 = q_ref[...]
    zero = jnp.zeros((), q2.dtype)
    qst = jnp.concatenate([jnp.where(lane < HEAD_DIM, q2, zero),
                           jnp.where(lane >= HEAD_DIM, q2, zero)], axis=0)
    posq = q0 + lax.broadcasted_iota(jnp.int32, (TQ, 1), 0)
    posq2 = jnp.concatenate([posq, posq], axis=0)
    upper = _suffix_matrix(TKB)
    t_last = (q0 + TQ - 1) // TKB

    def tile(t, carry, acc, diagonal):
        k0 = pl.multiple_of(t * TKB, TKB)
        z = _nt(qst, k_ref[pl.ds(k0, TKB), :])
        sp, lsig = _softplus_parts(z)
        if diagonal:
            causal = (k0 + lax.broadcasted_iota(jnp.int32, (1, TKB), 1)) < posq2
            sp = jnp.where(causal, sp, 0.0)
        hi, lo = _split_hi_lo(sp)
        later = _mm(hi, upper) + _mm(lo, upper) + carry
        a = jnp.exp(lsig - later)
        if diagonal:
            a = jnp.where(causal, a, 0.0)
        acc = acc + _mm(a.astype(BF16), v_ref[pl.ds(k0, TKB), :])
        carry = carry + jnp.sum(sp, axis=-1, keepdims=True)
        return carry, acc

    carry, acc = tile(t_last, jnp.zeros((2 * TQ, 1), F32), jnp.zeros((2 * TQ, LANES), F32), True)

    def body(it, c):
        return tile(t_last - 1 - it, c[0], c[1], False)

    carry, acc = lax.fori_loop(0, t_last, body, (carry, acc))
    o_ref[...] = jnp.where(lane < HEAD_DIM, acc[0:TQ], acc[TQ:2 * TQ]).astype(o_ref.dtype)


def _sb_prompt_call(sbq, sbkvb):
    bsz, t, _ = sbq.shape
    nslab = SB_WIDTH // LANES
    return pl.pallas_call(
        _sb_prompt_body,
        out_shape=jax.ShapeDtypeStruct((bsz, t, SB_WIDTH), BF16),
        grid=(bsz, nslab, t // TQ),
        in_specs=[pl.BlockSpec((None, TQ, LANES), lambda b, s, i: (b, i, s)),
                  pl.BlockSpec((None, t, LANES), lambda b, s, i: (b, 0, s)),
                  pl.BlockSpec((None, t, LANES), lambda b, s, i: (b, 0, nslab + s))],
        out_specs=pl.BlockSpec((None, TQ, LANES), lambda b, s, i: (b, i, s)),
        compiler_params=_cparams(("parallel", "parallel", "parallel")),
        name="sb_prompt",
    )(sbq, sbkvb, sbkvb)


def _pad_rows(x, n):
    return jnp.concatenate([x, jnp.zeros((n - x.shape[0], x.shape[1]), x.dtype)], axis=0)


def _sample_attn_body(pt_ref, q_ref, gn_ref, nsan_ref, winn_ref, sbq_ref, sbn_ref, wc_ref, win_ref,
                      *rest, n_pages, page):
    nsa_pages = rest[:n_pages]
    sb_pages = rest[n_pages:2 * n_pages]
    on_ref, os_ref, wout_ref, kc_s, vc_s = rest[2 * n_pages:]
    del pt_ref
    tn = q_ref.shape[0]
    nrow = N_KV_NSA * GROUP_SIZE * tn
    n_past = n_pages * page
    per_page = page // CMP_BLOCK
    ncmp = n_pages * per_page
    nslc_past = n_past // SLC_BLOCK
    lane = lax.broadcasted_iota(jnp.int32, (1, LANES), 1)
    row = lax.broadcasted_iota(jnp.int32, (nrow, 1), 0)
    tq_row = row & (tn - 1)

    qs = q_ref[...].astype(F32)
    qb = jnp.concatenate(
        [jnp.where((lane < HEAD_DIM) if g == 0 else (lane >= HEAD_DIM), qs[:, s * LANES:(s + 1) * LANES], 0.0)
         for g in range(N_KV_NSA) for s in range(4)], axis=0).astype(BF16)

    kc_s[...] = jnp.zeros(kc_s.shape, F32)
    vc_s[...] = jnp.zeros(vc_s.shape, F32)
    half = ncmp // 2
    for j in range(n_pages):
        pg = nsa_pages[j]
        kcj = jnp.sum(pg[:, 0:128].reshape(per_page, CMP_BLOCK, LANES) * wc_ref[0][None], axis=1)
        vcj = jnp.sum(pg[:, 128:256].reshape(per_page, CMP_BLOCK, LANES) * wc_ref[1][None], axis=1)
        for r in range(per_page):
            n = j * per_page + r
            dst = n // 2 + (half if n % 2 else 0)
            kc_s[dst:dst + 1, :] = kcj[r:r + 1, :]
            vc_s[dst:dst + 1, :] = vcj[r:r + 1, :]
    mc = lane < ncmp
    pc = _masked_softmax(_nt(qb, kc_s[...].astype(BF16)), mc)
    oc = _mm(pc.astype(BF16), vc_s[...].astype(BF16))
    gq = GROUP_SIZE * tn
    imp = jnp.concatenate(
        [sum(pc[g * gq + s * tn:g * gq + (s + 1) * tn] for s in range(4)) for g in range(N_KV_NSA)],
        axis=0)
    imp = imp + pltpu.roll(imp, LANES - half, 1)
    n_free = N_SEL - 3
    cand = (lane >= 1) & (lane <= nslc_past - 2)
    score = jnp.where(cand, imp, -jnp.inf)
    rank = jnp.zeros(score.shape, F32)
    for r in range(1, nslc_past):
        rank = rank + jnp.where(pltpu.roll(score, r, 1) >= score, 1.0, 0.0)
        rank = rank + jnp.where(pltpu.roll(score, LANES - r, 1) > score, 1.0, 0.0)
    sel = (cand & (rank < n_free)) | (lane == 0) | (lane == nslc_past - 1)
    sel = jnp.where(sel, 1.0, 0.0)
    selrows = jnp.concatenate([sel[g * tn:(g + 1) * tn] for g in range(N_KV_NSA) for _ in range(4)],
                              axis=0).astype(BF16)

    brow = lax.broadcasted_iota(jnp.int32, (LANES, page), 0)
    kcol = lax.broadcasted_iota(jnp.int32, (LANES, page), 1)
    new_ok = (lane <= tq_row) & (lane < tn)
    s_tiles = []
    for j in range(n_pages):
        s = _nt(qb, nsa_pages[j][:, 256:384].astype(BF16))
        expand = jnp.where(brow == ((j * page + kcol) >> 6), 1.0, 0.0).astype(BF16)
        s_tiles.append(jnp.where(_mm(selrows, expand) > 0.5, s, NEG_INF))
    ksn = _pad_rows(nsan_ref[:, 256:384], LANES).astype(BF16)
    vsn = _pad_rows(nsan_ref[:, 384:512], LANES).astype(BF16)
    s_tiles.append(jnp.where(new_ok, _nt(qb, ksn), NEG_INF))
    mx = functools.reduce(jnp.maximum, [jnp.max(s, axis=-1, keepdims=True) for s in s_tiles])
    p_tiles = [jnp.exp(s - mx) for s in s_tiles]
    den = sum(jnp.sum(p, axis=-1, keepdims=True) for p in p_tiles)
    osel = _mm(p_tiles[n_pages].astype(BF16), vsn)
    for j in range(n_pages):
        osel = osel + _mm(p_tiles[j].astype(BF16), nsa_pages[j][:, 384:512].astype(BF16))
    osel = osel / den

    nwin = win_ref.shape[0]
    wcol = lax.broadcasted_iota(jnp.int32, (1, nwin), 1)
    dist = (nwin + tq_row) - wcol
    mwin = (dist >= 0) & (dist < WINDOW)
    s_w = jnp.where(mwin, _nt(qb, win_ref[:, 0:128].astype(BF16)), NEG_INF)
    kwn = _pad_rows(winn_ref[:, 0:128], LANES).astype(BF16)
    vwn = _pad_rows(winn_ref[:, 128:256], LANES).astype(BF16)
    s_wn = jnp.where(new_ok, _nt(qb, kwn), NEG_INF)
    mxw = jnp.maximum(jnp.max(s_w, axis=-1, keepdims=True), jnp.max(s_wn, axis=-1, keepdims=True))
    p_w = jnp.exp(s_w - mxw)
    p_wn = jnp.exp(s_wn - mxw)
    denw = jnp.sum(p_w, axis=-1, keepdims=True) + jnp.sum(p_wn, axis=-1, keepdims=True)
    ow = (_mm(p_w.astype(BF16), win_ref[:, 128:256].astype(BF16)) + _mm(p_wn.astype(BF16), vwn)) / denw

    gates = gn_ref[...]
    for s in range(4):
        parts = []
        for g in range(N_KV_NSA):
            hd = s + 4 * g
            r = slice(g * gq + s * tn, g * gq + (s + 1) * tn)
            parts.append(gates[:, 3 * hd:3 * hd + 1] * oc[r] + gates[:, 3 * hd + 1:3 * hd + 2] * osel[r]
                         + gates[:, 3 * hd + 2:3 * hd + 3] * ow[r])
        on_ref[:, s * LANES:(s + 1) * LANES] = jnp.where(lane < HEAD_DIM, parts[0], parts[1])

    wout_ref[0:nwin - tn, :] = win_ref[tn:nwin, :]
    wout_ref[nwin - tn:nwin, :] = winn_ref[...]

    nh = N_HEADS_SB
    hrow = lax.broadcasted_iota(jnp.int32, (nh * tn, 1), 0)
    lane_w = lax.broadcasted_iota(jnp.int32, (1, SB_WIDTH), 1)
    sq = sbq_ref[...].astype(F32)
    qbd = jnp.where((lane_w >> 6) == (hrow >> 3), jnp.concatenate([sq] * nh, axis=0), 0.0).astype(BF16)
    tq_h = hrow & (tn - 1)
    upper = _suffix_matrix(page)
    kn = _pad_rows(sbn_ref[:, 0:SB_WIDTH], page).astype(BF16)
    vn = _pad_rows(sbn_ref[:, SB_WIDTH:], page).astype(BF16)
    lane_p = lax.broadcasted_iota(jnp.int32, (1, page), 1)
    causal = lane_p < tq_h
    sp, lsig = _softplus_parts(_nt(qbd, kn))
    sp = jnp.where(causal, sp, 0.0)
    hi, lo = _split_hi_lo(sp)
    later = _mm(hi, upper) + _mm(lo, upper)
    a = jnp.where(causal, jnp.exp(lsig - later), 0.0)
    acc = _mm(a.astype(BF16), vn)
    carry = jnp.sum(sp, axis=-1, keepdims=True)
    for j in range(n_pages - 1, -1, -1):
        pg = sb_pages[j]
        sp, lsig = _softplus_parts(_nt(qbd, pg[:, 0:SB_WIDTH].astype(BF16)))
        hi, lo = _split_hi_lo(sp)
        later = _mm(hi, upper) + _mm(lo, upper) + carry
        acc = acc + _mm(jnp.exp(lsig - later).astype(BF16), pg[:, SB_WIDTH:].astype(BF16))
        carry = carry + jnp.sum(sp, axis=-1, keepdims=True)
    o = jnp.zeros((tn, SB_WIDTH), F32)
    for hd in range(nh):
        o = o + jnp.where((lane_w >> 6) == hd, acc[hd * tn:(hd + 1) * tn], 0.0)
    os_ref[...] = o


def _sample_attn_call(page_table, q, gn, nsa_new, win_new, sbq, sb_new, wc_t, state_win, cache_nsa,
                      cache_sb, layer):
    nseq, tn, _ = q.shape
    n_pages = page_table.shape[1]
    page = cache_nsa.shape[2]
    nwin = state_win.shape[2]
    assert (n_pages * page) % SLC_BLOCK == 0 and tn == 8 and nwin == WINDOW and SLC_BLOCK == 64
    assert (n_pages * page) // SLC_BLOCK - 2 >= N_SEL - 3
    tok = lambda w: pl.BlockSpec((None, tn, w), lambda b, pt: (b, 0, 0))

    def page_spec(w, j):
        return pl.BlockSpec((None, None, page, w), lambda b, pt, j=j: (layer, pt[b, j], 0, 0))

    in_specs = ([tok(512), tok(128), tok(512), tok(256), tok(512), tok(1024),
                 pl.BlockSpec((None, 2, CMP_BLOCK, LANES), lambda b, pt: (layer, 0, 0, 0)),
                 pl.BlockSpec((None, None, nwin, 256), lambda b, pt: (layer, b, 0, 0))]
                + [page_spec(512, j) for j in range(n_pages)]
                + [page_spec(1024, j) for j in range(n_pages)])
    out_shape = [jax.ShapeDtypeStruct((nseq, tn, 512), F32), jax.ShapeDtypeStruct((nseq, tn, 512), F32),
                 jax.ShapeDtypeStruct((nseq, nwin, 256), F32)]
    out_specs = [tok(512), tok(512), pl.BlockSpec((None, nwin, 256), lambda b, pt: (b, 0, 0))]
    grid_spec = pltpu.PrefetchScalarGridSpec(
        num_scalar_prefetch=1, grid=(nseq,), in_specs=in_specs, out_specs=out_specs,
        scratch_shapes=[pltpu.VMEM((LANES, LANES), F32), pltpu.VMEM((LANES, LANES), F32)])
    return pl.pallas_call(
        functools.partial(_sample_attn_body, n_pages=n_pages, page=page),
        out_shape=out_shape,
        grid_spec=grid_spec,
        compiler_params=_cparams(("parallel",)),
        name="sample_attn",
    )(page_table, q, gn, nsa_new, win_new, sbq, sb_new, wc_t, state_win,
      *([cache_nsa] * n_pages), *([cache_sb] * n_pages))


def _post_body(x_ref, gt_ref, on_ref, os_ref, gm_ref, wa_ref, wb_ref, wo_ref, o_ref):
    ya = _mm(on_ref[...].astype(BF16), wa_ref[...])
    yb = _mm(os_ref[...].astype(BF16), wb_ref[...])
    m = gm_ref[:, 0:D_MODEL] * ya + gm_ref[:, D_MODEL:] * yb
    o_ref[...] = x_ref[...] + gt_ref[...] * _mm(m.astype(BF16), wo_ref[...])


def _post_call(x, gt, on, osb, gm, wa, wb, wo, layer, *, tm):
    bsz, t, _ = x.shape
    mrows = gt.shape[1]
    mblk = 1 if mrows == 1 else tm
    mod_map = (lambda b, i: (b, 0, 0)) if mrows == 1 else (lambda b, i: (b, i, 0))
    tok = lambda w: pl.BlockSpec((None, tm, w), lambda b, i: (b, i, 0))
    wspec = lambda k: pl.BlockSpec((None, k, D_MODEL), lambda b, i: (layer, 0, 0))
    return pl.pallas_call(
        _post_body,
        out_shape=jax.ShapeDtypeStruct(x.shape, F32),
        grid=(bsz, t // tm),
        in_specs=[tok(D_MODEL), pl.BlockSpec((None, mblk, D_MODEL), mod_map), tok(512), tok(512),
                  tok(2048), wspec(512), wspec(512), wspec(D_MODEL)],
        out_specs=tok(D_MODEL),
        compiler_params=_cparams(("parallel", "parallel")),
        name="post_attn",
    )(x, gt, on, osb, gm, wa, wb, wo)


def _moe_body(x_ref, sc_ref, sh_ref, gt_ref, g_ref, wr_ref, br_ref, w1_ref, w3_ref, w2_ref, o_ref,
              h_s, gate_s, acc_s):
    e = pl.program_id(2)
    lane = lax.broadcasted_iota(jnp.int32, (1, LANES), 1)
    lanef = lane.astype(F32)

    @pl.when(e == 0)
    def _():
        x = x_ref[...]
        ms = jnp.mean(x * x, axis=-1, keepdims=True)
        h = (x * lax.rsqrt(ms + EPS)) * g_ref[...]
        h = h * (1.0 + sc_ref[...]) + sh_ref[...]
        hb = h.astype(BF16)
        h_s[...] = hb
        logit = _mm(hb, wr_ref[...]) + br_ref[...]
        is_g = lane < N_GROUPS
        lg = jnp.where(is_g, logit, -jnp.inf)
        gmax = jnp.max(lg, axis=-1, keepdims=True)
        grp = jnp.min(jnp.where(lg == gmax, lanef, float(LANES)), axis=-1, keepdims=True)
        p_grp = 1.0 / jnp.sum(jnp.exp(lg - gmax), axis=-1, keepdims=True)
        lo_l = N_GROUPS + grp * EXPERTS_PER_GROUP
        in_grp = (lanef >= lo_l) & (lanef < lo_l + EXPERTS_PER_GROUP)
        le = jnp.where(in_grp, logit, -jnp.inf)
        v1 = jnp.max(le, axis=-1, keepdims=True)
        i1 = jnp.min(jnp.where(le == v1, lanef, float(LANES)), axis=-1, keepdims=True)
        le2 = jnp.where(lanef == i1, -jnp.inf, le)
        v2 = jnp.max(le2, axis=-1, keepdims=True)
        i2 = jnp.min(jnp.where(le2 == v2, lanef, float(LANES)), axis=-1, keepdims=True)
        e2 = jnp.exp(v2 - v1)
        w1 = p_grp * (1.0 / (1.0 + e2))
        w2 = p_grp * (e2 / (1.0 + e2))
        gate_s[...] = jnp.where(lanef == i1, w1, jnp.where(lanef == i2, w2, 0.0))
        acc_s[...] = jnp.zeros(acc_s.shape, F32)

    hb = h_s[...]
    ge = jnp.sum(jnp.where(lane == N_GROUPS + e, gate_s[...], 0.0), axis=-1, keepdims=True)
    u = _mm(hb, w1_ref[...])
    hid = (u * jax.nn.sigmoid(u)) * _mm(hb, w3_ref[...]) * ge
    acc_s[...] += _mm(hid.astype(BF16), w2_ref[...])

    @pl.when(e == pl.num_programs(2) - 1)
    def _():
        o_ref[...] = x_ref[...] + gt_ref[...] * acc_s[...]


def _moe_call(x, sc, sh, gt, g_norm, wr, br, w1, w3, w2, layer, *, tm):
    bsz, t, _ = x.shape
    mrows = sc.shape[1]
    mblk = 1 if mrows == 1 else tm
    mod_map = (lambda b, i, e: (b, 0, 0)) if mrows == 1 else (lambda b, i, e: (b, i, 0))
    tok = pl.BlockSpec((None, tm, D_MODEL), lambda b, i, e: (b, i, 0))
    mod = pl.BlockSpec((None, mblk, D_MODEL), mod_map)
    return pl.pallas_call(
        _moe_body,
        out_shape=jax.ShapeDtypeStruct(x.shape, F32),
        grid=(bsz, t // tm, N_EXPERTS),
        in_specs=[tok, mod, mod, mod,
                  pl.BlockSpec((None, None, 1, D_MODEL), lambda b, i, e: (layer, 1, 0, 0)),
                  pl.BlockSpec((None, D_MODEL, LANES), lambda b, i, e: (layer, 0, 0)),
                  pl.BlockSpec((None, 1, LANES), lambda b, i, e: (layer, 0, 0)),
                  pl.BlockSpec((None, None, D_MODEL, D_EXPERT), lambda b, i, e: (layer, e, 0, 0)),
                  pl.BlockSpec((None, None, D_MODEL, D_EXPERT), lambda b, i, e: (layer, e, 0, 0)),
                  pl.BlockSpec((None, None, D_EXPERT, D_MODEL), lambda b, i, e: (layer, e, 0, 0))],
        out_specs=tok,
        scratch_shapes=[pltpu.VMEM((tm, D_MODEL), BF16), pltpu.VMEM((tm, LANES), F32),
                        pltpu.VMEM((tm, D_MODEL), F32)],
        compiler_params=_cparams(("parallel", "parallel", "arbitrary")),
        name="moe",
    )(x, sc, sh, gt, g_norm, wr, br, w1, w3, w2)


def _rope_tables(pos):
    inv_freq = ROPE_THETA ** (-jnp.arange(0, HEAD_DIM, 2, dtype=F32) / HEAD_DIM)
    ang = pos.astype(F32)[:, None] * inv_freq[None, :]
    cos = jnp.cos(ang)
    sin = jnp.sin(ang)
    cos_t = jnp.concatenate([cos, cos, cos, cos], axis=1)
    sin_t = jnp.concatenate([-sin, sin, -sin, sin], axis=1)
    return cos_t, sin_t


def _even_odd(x):
    return jnp.concatenate([x[:, 0::2], x[:, 1::2]], axis=1)


def _tile(n, pref):
    return pref if n % pref == 0 else n


def _forward(x_prompt, x_sample, cache_nsa, cache_sb, state_win, page_table, c_prompt, c_sample,
             w_ada, b_ada, g_norm, w_in, w_qk_norm, w_cmp, w_br_a, w_br_b, w_out,
             w_route_group, b_route_group, w_route_expert, b_route_expert,
             w_expert_gate, w_expert_up, w_expert_down):
    depth = w_in.shape[0]
    bp, seq, _ = x_prompt.shape
    nseq, tn, _ = x_sample.shape
    n_phys, page = cache_nsa.shape[1], cache_nsa.shape[2]
    past_len = page_table.shape[1] * page

    pad_gn = jnp.zeros((depth, D_MODEL, LANES - 3 * N_HEADS_NSA), F32)
    w_in_p = jnp.concatenate(
        [w_in[:, :, _QPERM], w_in[:, :, 512:1280], w_in[:, :, 1280:1304], pad_gn, w_in[:, :, 1304:]],
        axis=-1).astype(BF16)
    qkn_t = jnp.concatenate([w_qk_norm, w_qk_norm], axis=-1)
    wc_t = jnp.concatenate([w_cmp, w_cmp], axis=-1)
    wa_p = w_br_a[:, _QPERM, :].astype(BF16)
    wb_p = w_br_b.astype(BF16)
    wo_p = w_out.astype(BF16)
    pad_r = jnp.zeros((depth, D_MODEL, LANES - N_GROUPS - N_EXPERTS), F32)
    wr_p = jnp.concatenate([w_route_group, w_route_expert, pad_r], axis=-1).astype(BF16)
    br_p = jnp.concatenate([b_route_group, b_route_expert,
                            jnp.zeros((depth, LANES - N_GROUPS - N_EXPERTS), F32)], axis=-1)
    br_p = br_p.reshape(depth, 1, LANES)
    w1_p = w_expert_gate.astype(BF16)
    w3_p = w_expert_up.astype(BF16)
    w2_p = w_expert_down.astype(BF16)
    g_norm4 = g_norm.reshape(depth, 2, 1, D_MODEL)
    r = lax.broadcasted_iota(jnp.int32, (2 * LANES, LANES), 0)
    c = lax.broadcasted_iota(jnp.int32, (2 * LANES, LANES), 1)
    bd = jnp.where((r % LANES) // HEAD_DIM == c // HEAD_DIM, 1.0, 0.0).astype(BF16)

    cos_p, sin_p = _rope_tables(jnp.arange(seq))
    cos_s, sin_s = _rope_tables(past_len + (jnp.arange(nseq * tn) % tn))

    cache_nsa_v = cache_nsa.reshape(depth, n_phys, page, 4 * N_KV_NSA * HEAD_DIM)
    cache_sb_v = cache_sb.reshape(depth, n_phys, page, 2 * N_HEADS_SB * HEAD_DIM)
    state_win_v = state_win.reshape(depth, nseq, state_win.shape[2], 2 * N_KV_NSA * HEAD_DIM)

    n_c = bp + nseq
    n_cp = -(-n_c // 8) * 8
    c_all = jnp.concatenate([c_prompt, c_sample, jnp.zeros((n_cp - n_c, D_MODEL), F32)], axis=0)
    mod = _ada_call(c_all, w_ada, b_ada)

    def mods(l):
        mp = mod[l, 0:bp].reshape(bp, 1, 6, D_MODEL)
        ms_ = jnp.broadcast_to(mod[l, bp:n_c].reshape(nseq, 1, 6, D_MODEL), (nseq, tn, 6, D_MODEL))
        ms_ = ms_.reshape(1, nseq * tn, 6, D_MODEL)
        return [mp[:, :, k] for k in range(6)], [ms_[:, :, k] for k in range(6)]

    xp = x_prompt
    xs = x_sample.reshape(1, nseq * tn, D_MODEL)
    tm_p = _tile(seq, 256)
    tm_s = _tile(nseq * tn, 256)
    outs = [[] for _ in range(6)]
    for l in range(depth):
        (sh1, sc1, gt1, sh2, sc2, gt2), (sh1s, sc1s, gt1s, sh2s, sc2s, gt2s) = mods(l)

        (q, nsa, nsab, win, winb, gn, sbq, sbkv, sbkvb, gm, kc, vc) = _inproj_call(
            xp, sc1, sh1, g_norm4, w_in_p, qkn_t, cos_p, sin_p, bd, wc_t, l, tm=tm_p, with_cmp=True,
            qdtype=BF16)
        o_n = _nsa_prompt_call(q, gn, _even_odd(kc).astype(BF16), _even_odd(vc).astype(BF16), nsab, winb)
        o_s = _sb_prompt_call(sbq, sbkvb)
        xp = _post_call(xp, gt1, o_n, o_s, gm, wa_p, wb_p, wo_p, l, tm=_tile(seq, 512))
        xp = _moe_call(xp, sc2, sh2, gt2, g_norm4, wr_p, br_p, w1_p, w3_p, w2_p, l, tm=_tile(seq, 1024))
        outs[0].append(nsa.reshape(bp, seq, 4, N_KV_NSA, HEAD_DIM))
        outs[1].append(sbkv.reshape(bp, seq, 2, N_HEADS_SB, HEAD_DIM))
        nw = min(WINDOW, seq)
        outs[2].append(win[:, seq - nw:].reshape(bp, nw, 2, N_KV_NSA, HEAD_DIM))

        (qs_, nsas, _, wins, _, gns, sbqs, sbkvs, _, gms) = _inproj_call(
            xs, sc1s, sh1s, g_norm4, w_in_p, qkn_t, cos_s, sin_s, bd, wc_t, l, tm=tm_s, with_cmp=False,
            qdtype=F32)
        sq = lambda a: a.reshape(nseq, tn, a.shape[-1])
        o_ns, o_ss, win_new = _sample_attn_call(
            page_table, sq(qs_), sq(gns), sq(nsas), sq(wins), sq(sbqs), sq(sbkvs), wc_t, state_win_v,
            cache_nsa_v, cache_sb_v, l)
        xs = _post_call(xs, gt1s, o_ns.reshape(1, nseq * tn, 512), o_ss.reshape(1, nseq * tn, 512), gms,
                        wa_p, wb_p, wo_p, l, tm=tm_s)
        xs = _moe_call(xs, sc2s, sh2s, gt2s, g_norm4, wr_p, br_p, w1_p, w3_p, w2_p, l, tm=tm_s)
        outs[3].append(nsas.reshape(nseq, tn, 4, N_KV_NSA, HEAD_DIM))
        outs[4].append(sbkvs.reshape(nseq, tn, 2, N_HEADS_SB, HEAD_DIM))
        outs[5].append(win_new.reshape(nseq, win_new.shape[1], 2, N_KV_NSA, HEAD_DIM))

    return (xp, xs.reshape(nseq, tn, D_MODEL)) + tuple(jnp.stack(o) for o in outs)


def kernel(x_prompt, x_sample, cache_nsa, cache_sb, state_win, page_table, c_prompt, c_sample, w_ada, b_ada, g_norm, w_in, w_qk_norm, w_cmp, w_br_a, w_br_b, w_out, w_route_group, b_route_group, w_route_expert, b_route_expert, w_expert_gate, w_expert_up, w_expert_down):
    return _forward(x_prompt, x_sample, cache_nsa, cache_sb, state_win, page_table, c_prompt, c_sample,
                    w_ada, b_ada, g_norm, w_in, w_qk_norm, w_cmp, w_br_a, w_br_b, w_out,
                    w_route_group, b_route_group, w_route_expert, b_route_expert,
                    w_expert_gate, w_expert_up, w_expert_down)
```

```python
import functools

import numpy as np
import jax
import jax.numpy as jnp
from jax import lax
from jax.experimental import pallas as pl
from jax.experimental.pallas import tpu as pltpu

F32 = jnp.float32
BF16 = jnp.bfloat16

D_MODEL = 1024
HEAD_DIM = 64
N_HEADS_NSA = 8
N_KV_NSA = 2
GROUP_SIZE = 4
N_HEADS_SB = 8
NSA_WIDTH = 512
SB_WIDTH = 512
CMP_BLOCK = 32
SLC_BLOCK = 64
N_SEL = 16
WINDOW = 512
ROPE_THETA = 10000.0
N_GROUPS = 4
EXPERTS_PER_GROUP = 4
N_EXPERTS = 16
D_EXPERT = 256
EPS = 1e-6
FORCE_SCORE = 1e4
NEG_INF = -1e30
SCALE = HEAD_DIM ** -0.5

LANES = 128
C_Q = 0
C_KV = 512
C_GN = 1280
C_SB = 1408
C_GM = 2944
C_END = 4992
VMEM_LIMIT = 56 * 1024 * 1024

_QPERM = np.concatenate([np.arange(h * 64, (h + 1) * 64) for s in range(4) for h in (s, 4 + s)])


def _nt(a, b):
    return lax.dot_general(a, b, (((1,), (1,)), ((), ())), preferred_element_type=F32)


def _mm(a, b):
    return jnp.dot(a, b, preferred_element_type=F32)


def _split_hi_lo(x):
    hi = x.astype(BF16)
    lo = (x - hi.astype(F32)).astype(BF16)
    return hi, lo


def _cparams(sem):
    return pltpu.CompilerParams(dimension_semantics=sem, vmem_limit_bytes=VMEM_LIMIT)


def _ada_body(c_ref, w_ref, b_ref, o_ref):
    c = c_ref[...]
    a = (c * jax.nn.sigmoid(c)).astype(BF16)
    o_ref[...] = _mm(a, w_ref[...].astype(BF16)) + b_ref[...]


def _ada_call(c_all, w_ada, b_ada):
    depth = w_ada.shape[0]
    n = c_all.shape[0]
    tn = 1536
    return pl.pallas_call(
        _ada_body,
        out_shape=jax.ShapeDtypeStruct((depth, n, 6 * D_MODEL), F32),
        grid=(depth, 6 * D_MODEL // tn),
        in_specs=[pl.BlockSpec((n, D_MODEL), lambda l, j: (0, 0)),
                  pl.BlockSpec((None, D_MODEL, tn), lambda l, j: (l, 0, j)),
                  pl.BlockSpec((None, 1, tn), lambda l, j: (l, 0, j))],
        out_specs=pl.BlockSpec((None, n, tn), lambda l, j: (l, 0, j)),
        compiler_params=_cparams(("parallel", "parallel")),
        name="ada_mod",
    )(c_all, w_ada, b_ada.reshape(depth, 1, 6 * D_MODEL))


def _inproj_body(x_ref, sc_ref, sh_ref, g_ref, w_ref, qkn_ref, cos_ref, sin_ref, bd_ref, wc_ref,
                 *outs, with_cmp):
    if with_cmp:
        (q_ref, nsa_ref, nsab_ref, win_ref, winb_ref, gn_ref, sbq_ref, sbkv_ref, sbkvb_ref,
         gm_ref, kc_ref, vc_ref) = outs
    else:
        (q_ref, nsa_ref, nsab_ref, win_ref, winb_ref, gn_ref, sbq_ref, sbkv_ref, sbkvb_ref,
         gm_ref) = outs
    x = x_ref[...]
    tm = x.shape[0]
    ms = jnp.mean(x * x, axis=-1, keepdims=True)
    h = (x * lax.rsqrt(ms + EPS)) * g_ref[...]
    h = h * (1.0 + sc_ref[...]) + sh_ref[...]
    hb = h.astype(BF16)
    cos = cos_ref[...]
    sin = sin_ref[...]
    bd = bd_ref[...]
    lane = lax.broadcasted_iota(jnp.int32, (1, LANES), 1)
    first_half = (lane & (HEAD_DIM - 1)) < HEAD_DIM // 2

    def seg(lo, hi):
        return _mm(hb, w_ref[:, lo:hi])

    def norm_rope(xs, gq):
        hi, lo = _split_hi_lo(xs * xs)
        ssum = _mm(jnp.concatenate([hi, lo], axis=1), bd)
        y = xs * lax.rsqrt(ssum * (1.0 / HEAD_DIM) + EPS) * gq
        sw = jnp.where(first_half, pltpu.roll(y, 96, 1), pltpu.roll(y, 32, 1))
        return y * cos + sw * sin

    pq = seg(C_Q, C_KV)
    for s in range(4):
        qs = norm_rope(pq[:, s * LANES:(s + 1) * LANES], qkn_ref[0:1, :]) * SCALE
        q_ref[:, s * LANES:(s + 1) * LANES] = qs.astype(q_ref.dtype)

    pkv = seg(C_KV, C_GN)
    kc = norm_rope(pkv[:, 0:128], qkn_ref[1:2, :])
    vc = pkv[:, 128:256]
    ks = norm_rope(pkv[:, 256:384], qkn_ref[2:3, :])
    vs = pkv[:, 384:512]
    kw = norm_rope(pkv[:, 512:640], qkn_ref[3:4, :])
    vw = pkv[:, 640:768]
    for j, v in enumerate((kc, vc, ks, vs)):
        nsa_ref[:, j * LANES:(j + 1) * LANES] = v
        nsab_ref[:, j * LANES:(j + 1) * LANES] = v.astype(BF16)
    for j, v in enumerate((kw, vw)):
        win_ref[:, j * LANES:(j + 1) * LANES] = v
        winb_ref[:, j * LANES:(j + 1) * LANES] = v.astype(BF16)
    if with_cmp:
        nb = tm // CMP_BLOCK
        kc_ref[...] = jnp.sum(kc.reshape(nb, CMP_BLOCK, LANES) * wc_ref[0][None], axis=1)
        vc_ref[...] = jnp.sum(vc.reshape(nb, CMP_BLOCK, LANES) * wc_ref[1][None], axis=1)

    gn_ref[...] = jax.nn.sigmoid(seg(C_GN, C_SB))

    psb = seg(C_SB, C_GM)
    sbq_ref[...] = (psb[:, 0:SB_WIDTH] * SCALE).astype(sbq_ref.dtype)
    sbkv_ref[...] = psb[:, SB_WIDTH:]
    sbkvb_ref[...] = psb[:, SB_WIDTH:].astype(BF16)

    gm_ref[...] = jax.nn.sigmoid(seg(C_GM, C_END))


def _inproj_call(x, sc, sh, g_norm, w_in_p, qkn_t, cos_t, sin_t, bd, wc_t, layer, *, tm, with_cmp,
                 qdtype):
    bsz, t, _ = x.shape
    mrows = sc.shape[1]
    mblk = 1 if mrows == 1 else tm
    mod_map = (lambda b, i: (b, 0, 0)) if mrows == 1 else (lambda b, i: (b, i, 0))
    tok = lambda w: pl.BlockSpec((None, tm, w), lambda b, i: (b, i, 0))
    in_specs = [
        tok(D_MODEL),
        pl.BlockSpec((None, mblk, D_MODEL), mod_map),
        pl.BlockSpec((None, mblk, D_MODEL), mod_map),
        pl.BlockSpec((None, None, 1, D_MODEL), lambda b, i: (layer, 0, 0, 0)),
        pl.BlockSpec((None, D_MODEL, C_END), lambda b, i: (layer, 0, 0)),
        pl.BlockSpec((None, 4, LANES), lambda b, i: (layer, 0, 0)),
        pl.BlockSpec((tm, LANES), lambda b, i: (i, 0)),
        pl.BlockSpec((tm, LANES), lambda b, i: (i, 0)),
        pl.BlockSpec((2 * LANES, LANES), lambda b, i: (0, 0)),
        pl.BlockSpec((None, 2, CMP_BLOCK, LANES), lambda b, i: (layer, 0, 0, 0)),
    ]
    sds = lambda w, dt: jax.ShapeDtypeStruct((bsz, t, w), dt)
    out_shape = [sds(512, qdtype), sds(512, F32), sds(512, BF16), sds(256, F32), sds(256, BF16),
                 sds(128, F32), sds(512, qdtype), sds(1024, F32), sds(1024, BF16), sds(2048, F32)]
    out_specs = [tok(512), tok(512), tok(512), tok(256), tok(256), tok(128), tok(512), tok(1024),
                 tok(1024), tok(2048)]
    if with_cmp:
        nb = tm // CMP_BLOCK
        out_shape += [jax.ShapeDtypeStruct((bsz, t // CMP_BLOCK, LANES), F32)] * 2
        out_specs += [pl.BlockSpec((None, nb, LANES), lambda b, i: (b, i, 0))] * 2
    return pl.pallas_call(
        functools.partial(_inproj_body, with_cmp=with_cmp),
        out_shape=out_shape,
        grid=(bsz, t // tm),
        in_specs=in_specs,
        out_specs=out_specs,
        compiler_params=_cparams(("parallel", "parallel")),
        name="inproj",
    )(x, sc, sh, g_norm, w_in_p, qkn_t, cos_t, sin_t, bd, wc_t)


TQ = 128
TKS = 512


def _masked_softmax(s, mask):
    sm = jnp.where(mask, s, NEG_INF)
    mx = jnp.max(sm, axis=-1, keepdims=True)
    e = jnp.exp(sm - mx)
    den = jnp.sum(e, axis=-1, keepdims=True)
    return jnp.where(mask, e / den, 0.0)


def _top_sel_rows(score, n_pick):
    nbk = score.shape[0]
    jrow = lax.broadcasted_iota(jnp.int32, score.shape, 0).astype(F32)
    sel = jnp.zeros(score.shape, F32)
    for _ in range(n_pick):
        m = jnp.max(score, axis=0, keepdims=True)
        idx = jnp.min(jnp.where(score == m, jrow, float(nbk)), axis=0, keepdims=True)
        pick = jrow == idx
        sel = jnp.where(pick, 1.0, sel)
        score = jnp.where(pick, -jnp.inf, score)
    return sel


def _nsa_prompt_body(q_ref, gn_ref, kc_ref, vc_ref, ks_ref, vs_ref, kw_ref, vw_ref, o_ref,
                     s_buf, m_ref, l_ref, acc_ref, oc_ref, sel_ref):
    i = pl.program_id(1)
    q0 = i * TQ
    ncmp = kc_ref.shape[0]
    nslc = ncmp // 2
    lane = lax.broadcasted_iota(jnp.int32, (1, LANES), 1)
    gates = gn_ref[...]
    posq = q0 + lax.broadcasted_iota(jnp.int32, (TQ, 1), 0)
    posq4 = q0 + (lax.broadcasted_iota(jnp.int32, (GROUP_SIZE * TQ, 1), 0) & (TQ - 1))

    lc = lax.broadcasted_iota(jnp.int32, (1, ncmp), 1)
    blk = jnp.where(lc < nslc, 2 * lc, 2 * (lc - nslc) + 1)
    mc4 = ((blk + 1) * CMP_BLOCK - 1) <= posq4

    jrow = lax.broadcasted_iota(jnp.int32, (nslc, TQ), 0)
    cur = (q0 + lax.broadcasted_iota(jnp.int32, (nslc, TQ), 1)) >> 6
    forced = (jrow == 0) | (jrow == cur) | (jrow == cur - 1)
    valid = jrow <= cur

    wlen = WINDOW + TQ
    wstart = pl.multiple_of(jnp.maximum(q0 - WINDOW, 0), TQ)
    posw = wstart + lax.broadcasted_iota(jnp.int32, (1, wlen), 1)
    dist = posq4 - posw
    mw4 = (dist >= 0) & (dist < WINDOW)

    n_tiles = (q0 + TQ + TKS - 1) // TKS
    lmasks = [lane < HEAD_DIM, lane >= HEAD_DIM]
    zero = jnp.zeros((), q_ref.dtype)

    def q_group(g):
        return jnp.concatenate(
            [jnp.where(lmasks[g], q_ref[:, s * LANES:(s + 1) * LANES], zero) for s in range(4)], axis=0)

    scores = []
    for g in range(N_KV_NSA):
        pc = _masked_softmax(_nt(q_group(g), kc_ref[...]), mc4)
        oc_ref[g] = _mm(pc.astype(BF16), vc_ref[...])
        imp = pc[0:TQ] + pc[TQ:2 * TQ] + pc[2 * TQ:3 * TQ] + pc[3 * TQ:4 * TQ]
        imp = imp[:, 0:nslc] + imp[:, nslc:ncmp]
        if nslc < LANES:
            imp = jnp.concatenate([imp, jnp.zeros((TQ, LANES - nslc), F32)], axis=1)
        imp_t = imp.T[0:nslc, :]
        scores.append(jnp.where(forced, FORCE_SCORE, jnp.where(valid, imp_t, -1.0)))
    sel_t = _top_sel_rows(jnp.concatenate(scores, axis=1), min(N_SEL, nslc))
    for g in range(N_KV_NSA):
        st = sel_t[:, g * TQ:(g + 1) * TQ]
        if nslc < LANES:
            st = jnp.concatenate([st, jnp.zeros((LANES - nslc, TQ), F32)], axis=0)
        sel_ref[g] = st.T.astype(BF16)

    s_buf[...] = jnp.full(s_buf.shape, NEG_INF, F32)
    m_ref[...] = jnp.full(m_ref.shape, NEG_INF, F32)
    l_ref[...] = jnp.zeros(l_ref.shape, F32)
    acc_ref[...] = jnp.zeros(acc_ref.shape, F32)

    def sel_trip(it, c):
        kb0 = pl.multiple_of(jnp.maximum(it - 1, 0) * TKS, TKS)
        ka0 = pl.multiple_of(jnp.minimum(it, n_tiles - 1) * TKS, TKS)
        kpos = ka0 + lax.broadcasted_iota(jnp.int32, (LANES, TKS), 1)
        brow = lax.broadcasted_iota(jnp.int32, (LANES, TKS), 0)
        expand = jnp.where(brow == (kpos >> 6), 1.0, 0.0).astype(BF16)
        causal = kpos[0:1, :] <= posq
        for g in range(N_KV_NSA):
            sa = _nt(q_group(g), ks_ref[pl.ds(ka0, TKS), :])
            picked = _mm(sel_ref[g], expand)
            bias = jnp.where((picked > 0.5) & causal, 0.0, NEG_INF)

            s = s_buf[g]
            m_old = m_ref[g]
            m_new = jnp.maximum(m_old, jnp.max(s, axis=-1, keepdims=True))
            alpha = jnp.exp(m_old - m_new)
            p = jnp.exp(s - m_new)
            l_ref[g] = alpha * l_ref[g] + jnp.sum(p, axis=-1, keepdims=True)
            m_ref[g] = m_new
            acc_ref[g] = alpha * acc_ref[g] + _mm(p.astype(BF16), vs_ref[pl.ds(kb0, TKS), :])
            s_buf[g] = jnp.concatenate(
                [sa[r * TQ:(r + 1) * TQ] + bias for r in range(GROUP_SIZE)], axis=0)
        return c

    lax.fori_loop(0, n_tiles + 1, sel_trip, 0)

    out = [jnp.zeros((TQ, LANES), F32) for _ in range(4)]
    for g in range(N_KV_NSA):
        osel = acc_ref[g] / l_ref[g]
        oc = oc_ref[g]
        pw = _masked_softmax(_nt(q_group(g), kw_ref[pl.ds(wstart, wlen), :]), mw4)
        ow = _mm(pw.astype(BF16), vw_ref[pl.ds(wstart, wlen), :])
        for s in range(4):
            hd = s + 4 * g
            r = slice(s * TQ, (s + 1) * TQ)
            comb = (gates[:, 3 * hd:3 * hd + 1] * oc[r] + gates[:, 3 * hd + 1:3 * hd + 2] * osel[r]
                    + gates[:, 3 * hd + 2:3 * hd + 3] * ow[r])
            out[s] = jnp.where(lmasks[g], comb, out[s])
    for s in range(4):
        o_ref[:, s * LANES:(s + 1) * LANES] = out[s].astype(o_ref.dtype)


def _nsa_prompt_call(q, gn, kcp, vcp, nsab, winb):
    bsz, t, _ = q.shape
    ncmp = kcp.shape[1]
    rows = GROUP_SIZE * TQ
    full = lambda col: pl.BlockSpec((None, t, LANES), lambda b, i, c=col: (b, 0, c))
    return pl.pallas_call(
        _nsa_prompt_body,
        out_shape=jax.ShapeDtypeStruct((bsz, t, NSA_WIDTH), BF16),
        grid=(bsz, t // TQ),
        in_specs=[pl.BlockSpec((None, TQ, NSA_WIDTH), lambda b, i: (b, i, 0)),
                  pl.BlockSpec((None, TQ, LANES), lambda b, i: (b, i, 0)),
                  pl.BlockSpec((None, ncmp, LANES), lambda b, i: (b, 0, 0)),
                  pl.BlockSpec((None, ncmp, LANES), lambda b, i: (b, 0, 0)),
                  full(2), full(3), full(0), full(1)],
        out_specs=pl.BlockSpec((None, TQ, NSA_WIDTH), lambda b, i: (b, i, 0)),
        scratch_shapes=[pltpu.VMEM((N_KV_NSA, rows, TKS), F32), pltpu.VMEM((N_KV_NSA, rows, 1), F32),
                        pltpu.VMEM((N_KV_NSA, rows, 1), F32), pltpu.VMEM((N_KV_NSA, rows, LANES), F32),
                        pltpu.VMEM((N_KV_NSA, rows, LANES), F32), pltpu.VMEM((N_KV_NSA, TQ, LANES), BF16)],
        compiler_params=_cparams(("parallel", "parallel")),
        name="nsa_prompt",
    )(q, gn, kcp, vcp, nsab, nsab, winb, winb)


TKB = 256


def _softplus_parts(z):
    sp = jnp.maximum(z, 0.0) + jnp.log(1.0 + jnp.exp(-jnp.abs(z)))
    return sp, z - sp


def _suffix_matrix(n):
    r = lax.broadcasted_iota(jnp.int32, (n, n), 0)
    c = lax.broadcasted_iota(jnp.int32, (n, n), 1)
    return jnp.where(r > c, 1.0, 0.0).astype(BF16)


SB_SLABS = 2


def _sb_prompt_body(q_ref, k_ref, v_ref, o_ref, z_buf, e_buf, g_buf, sp_buf, acc_ref):
    i = pl.program_id(2)
    q0 = i * TQ
    lane = lax.broadcasted_iota(jnp.int32, (1, LANES), 1)
    zero = jnp.zeros((), q_ref.dtype)
    posq2 = q0 + (lax.broadcasted_iota(jnp.int32, (2 * TQ, 1), 0) & (TQ - 1))
    upper = _suffix_matrix(TKB)
    t_last = (q0 + TQ - 1) // TKB

    def q_stack(c):
        q2 = q_ref[:, c * LANES:(c + 1) * LANES]
        return jnp.concatenate([jnp.where(lane < HEAD_DIM, q2, zero),
                                jnp.where(lane >= HEAD_DIM, q2, zero)], axis=0)

    def k_tile(c, t):
        return k_ref[pl.ds(pl.multiple_of(t * TKB, TKB), TKB), c * LANES:(c + 1) * LANES]

    def v_tile(c, t):
        return v_ref[pl.ds(pl.multiple_of(t * TKB, TKB), TKB), c * LANES:(c + 1) * LANES]

    causal = (t_last * TKB + lax.broadcasted_iota(jnp.int32, (1, TKB), 1)) < posq2
    run0 = []
    for c in range(SB_SLABS):
        sp, lsig = _softplus_parts(_nt(q_stack(c), k_tile(c, t_last)))
        sp = jnp.where(causal, sp, 0.0)
        a = jnp.where(causal, jnp.exp(lsig - _mm(sp.astype(BF16), upper)), 0.0)
        acc_ref[c] = _mm(a.astype(BF16), v_tile(c, t_last))
        run0.append(jnp.sum(sp, axis=-1, keepdims=True))

    n_rest = t_last
    z_buf[...] = jnp.zeros(z_buf.shape, F32)
    e_buf[...] = jnp.zeros(e_buf.shape, F32)
    g_buf[...] = jnp.zeros(g_buf.shape, F32)
    sp_buf[...] = jnp.zeros(sp_buf.shape, BF16)

    def body(it, runs):
        ta = jnp.clip(t_last - 1 - it, 0, t_last)
        jc = it - 3
        wc = jnp.where((jc >= 0) & (jc < n_rest), 1.0, 0.0)
        tc = jnp.clip(t_last - 1 - jc, 0, t_last)
        jb = it - 1
        wb = jnp.where((jb >= 0) & (jb < n_rest), 1.0, 0.0)
        new_runs = []
        for c in range(SB_SLABS):
            z_new = _nt(q_stack(c), k_tile(c, ta))
            acc_ref[c] += wc * _mm(jnp.exp(e_buf[c]).astype(BF16), v_tile(c, tc))
            e_buf[c] = g_buf[c] - _mm(sp_buf[c], upper)
            sp, lsig = _softplus_parts(z_buf[c])
            sp_buf[c] = sp.astype(BF16)
            g_buf[c] = lsig - runs[c]
            z_buf[c] = z_new
            new_runs.append(runs[c] + wb * jnp.sum(sp, axis=-1, keepdims=True))
        return tuple(new_runs)

    lax.fori_loop(0, jnp.where(n_rest > 0, n_rest + 3, 0), body, tuple(run0))
    for c in range(SB_SLABS):
        acc = acc_ref[c]
        o_ref[:, c * LANES:(c + 1) * LANES] = jnp.where(
            lane < HEAD_DIM, acc[0:TQ], acc[TQ:2 * TQ]).astype(o_ref.dtype)


def _sb_prompt_call(sbq, sbkvb):
    bsz, t, _ = sbq.shape
    w = SB_SLABS * LANES
    ngrp = SB_WIDTH // w
    buf = lambda last, dt: pltpu.VMEM((SB_SLABS, 2 * TQ, last), dt)
    return pl.pallas_call(
        _sb_prompt_body,
        out_shape=jax.ShapeDtypeStruct((bsz, t, SB_WIDTH), BF16),
        grid=(bsz, ngrp, t // TQ),
        in_specs=[pl.BlockSpec((None, TQ, w), lambda b, s, i: (b, i, s)),
                  pl.BlockSpec((None, t, w), lambda b, s, i: (b, 0, s)),
                  pl.BlockSpec((None, t, w), lambda b, s, i: (b, 0, ngrp + s))],
        out_specs=pl.BlockSpec((None, TQ, w), lambda b, s, i: (b, i, s)),
        scratch_shapes=[buf(TKB, F32), buf(TKB, F32), buf(TKB, F32), buf(TKB, BF16), buf(LANES, F32)],
        compiler_params=_cparams(("parallel", "parallel", "parallel")),
        name="sb_prompt",
    )(sbq, sbkvb, sbkvb)


def _pad_rows(x, n):
    return jnp.concatenate([x, jnp.zeros((n - x.shape[0], x.shape[1]), x.dtype)], axis=0)


def _sample_attn_body(pt_ref, q_ref, gn_ref, nsan_ref, winn_ref, sbq_ref, sbn_ref, wc_ref, win_ref,
                      *rest, n_pages, page):
    nsa_pages = rest[:n_pages]
    sb_pages = rest[n_pages:2 * n_pages]
    on_ref, os_ref, wout_ref, kc_s, vc_s = rest[2 * n_pages:]
    del pt_ref
    tn = q_ref.shape[0]
    nrow = N_KV_NSA * GROUP_SIZE * tn
    n_past = n_pages * page
    per_page = page // CMP_BLOCK
    ncmp = n_pages * per_page
    nslc_past = n_past // SLC_BLOCK
    lane = lax.broadcasted_iota(jnp.int32, (1, LANES), 1)
    row = lax.broadcasted_iota(jnp.int32, (nrow, 1), 0)
    tq_row = row & (tn - 1)

    qs = q_ref[...].astype(F32)
    qb = jnp.concatenate(
        [jnp.where((lane < HEAD_DIM) if g == 0 else (lane >= HEAD_DIM), qs[:, s * LANES:(s + 1) * LANES], 0.0)
         for g in range(N_KV_NSA) for s in range(4)], axis=0).astype(BF16)

    kc_s[...] = jnp.zeros(kc_s.shape, F32)
    vc_s[...] = jnp.zeros(vc_s.shape, F32)
    half = ncmp // 2
    for j in range(n_pages):
        pg = nsa_pages[j]
        kcj = jnp.sum(pg[:, 0:128].reshape(per_page, CMP_BLOCK, LANES) * wc_ref[0][None], axis=1)
        vcj = jnp.sum(pg[:, 128:256].reshape(per_page, CMP_BLOCK, LANES) * wc_ref[1][None], axis=1)
        for r in range(per_page):
            n = j * per_page + r
            dst = n // 2 + (half if n % 2 else 0)
            kc_s[dst:dst + 1, :] = kcj[r:r + 1, :]
            vc_s[dst:dst + 1, :] = vcj[r:r + 1, :]
    mc = lane < ncmp
    pc = _masked_softmax(_nt(qb, kc_s[...].astype(BF16)), mc)
    oc = _mm(pc.astype(BF16), vc_s[...].astype(BF16))
    gq = GROUP_SIZE * tn
    imp = jnp.concatenate(
        [sum(pc[g * gq + s * tn:g * gq + (s + 1) * tn] for s in range(4)) for g in range(N_KV_NSA)],
        axis=0)
    imp = imp + pltpu.roll(imp, LANES - half, 1)
    n_free = N_SEL - 3
    cand = (lane >= 1) & (lane <= nslc_past - 2)
    score = jnp.where(cand, imp, -jnp.inf)
    rank = jnp.zeros(score.shape, F32)
    for r in range(1, nslc_past):
        rank = rank + jnp.where(pltpu.roll(score, r, 1) >= score, 1.0, 0.0)
        rank = rank + jnp.where(pltpu.roll(score, LANES - r, 1) > score, 1.0, 0.0)
    sel = (cand & (rank < n_free)) | (lane == 0) | (lane == nslc_past - 1)
    sel = jnp.where(sel, 1.0, 0.0)
    selrows = jnp.concatenate([sel[g * tn:(g + 1) * tn] for g in range(N_KV_NSA) for _ in range(4)],
                              axis=0).astype(BF16)

    brow = lax.broadcasted_iota(jnp.int32, (LANES, page), 0)
    kcol = lax.broadcasted_iota(jnp.int32, (LANES, page), 1)
    new_ok = (lane <= tq_row) & (lane < tn)
    s_tiles = []
    for j in range(n_pages):
        s = _nt(qb, nsa_pages[j][:, 256:384].astype(BF16))
        expand = jnp.where(brow == ((j * page + kcol) >> 6), 1.0, 0.0).astype(BF16)
        s_tiles.append(jnp.where(_mm(selrows, expand) > 0.5, s, NEG_INF))
    ksn = _pad_rows(nsan_ref[:, 256:384], LANES).astype(BF16)
    vsn = _pad_rows(nsan_ref[:, 384:512], LANES).astype(BF16)
    s_tiles.append(jnp.where(new_ok, _nt(qb, ksn), NEG_INF))
    mx = functools.reduce(jnp.maximum, [jnp.max(s, axis=-1, keepdims=True) for s in s_tiles])
    p_tiles = [jnp.exp(s - mx) for s in s_tiles]
    den = sum(jnp.sum(p, axis=-1, keepdims=True) for p in p_tiles)
    osel = _mm(p_tiles[n_pages].astype(BF16), vsn)
    for j in range(n_pages):
        osel = osel + _mm(p_tiles[j].astype(BF16), nsa_pages[j][:, 384:512].astype(BF16))
    osel = osel / den

    nwin = win_ref.shape[0]
    wcol = lax.broadcasted_iota(jnp.int32, (1, nwin), 1)
    dist = (nwin + tq_row) - wcol
    mwin = (dist >= 0) & (dist < WINDOW)
    s_w = jnp.where(mwin, _nt(qb, win_ref[:, 0:128].astype(BF16)), NEG_INF)
    kwn = _pad_rows(winn_ref[:, 0:128], LANES).astype(BF16)
    vwn = _pad_rows(winn_ref[:, 128:256], LANES).astype(BF16)
    s_wn = jnp.where(new_ok, _nt(qb, kwn), NEG_INF)
    mxw = jnp.maximum(jnp.max(s_w, axis=-1, keepdims=True), jnp.max(s_wn, axis=-1, keepdims=True))
    p_w = jnp.exp(s_w - mxw)
    p_wn = jnp.exp(s_wn - mxw)
    denw = jnp.sum(p_w, axis=-1, keepdims=True) + jnp.sum(p_wn, axis=-1, keepdims=True)
    ow = (_mm(p_w.astype(BF16), win_ref[:, 128:256].astype(BF16)) + _mm(p_wn.astype(BF16), vwn)) / denw

    gates = gn_ref[...]
    for s in range(4):
        parts = []
        for g in range(N_KV_NSA):
            hd = s + 4 * g
            r = slice(g * gq + s * tn, g * gq + (s + 1) * tn)
            parts.append(gates[:, 3 * hd:3 * hd + 1] * oc[r] + gates[:, 3 * hd + 1:3 * hd + 2] * osel[r]
                         + gates[:, 3 * hd + 2:3 * hd + 3] * ow[r])
        on_ref[:, s * LANES:(s + 1) * LANES] = jnp.where(lane < HEAD_DIM, parts[0], parts[1])

    wout_ref[0:nwin - tn, :] = win_ref[tn:nwin, :]
    wout_ref[nwin - tn:nwin, :] = winn_ref[...]

    nh = N_HEADS_SB
    hrow = lax.broadcasted_iota(jnp.int32, (nh * tn, 1), 0)
    lane_w = lax.broadcasted_iota(jnp.int32, (1, SB_WIDTH), 1)
    sq = sbq_ref[...].astype(F32)
    qbd = jnp.where((lane_w >> 6) == (hrow >> 3), jnp.concatenate([sq] * nh, axis=0), 0.0).astype(BF16)
    tq_h = hrow & (tn - 1)
    upper = _suffix_matrix(page)
    kn = _pad_rows(sbn_ref[:, 0:SB_WIDTH], page).astype(BF16)
    vn = _pad_rows(sbn_ref[:, SB_WIDTH:], page).astype(BF16)
    lane_p = lax.broadcasted_iota(jnp.int32, (1, page), 1)
    causal = lane_p < tq_h
    sp, lsig = _softplus_parts(_nt(qbd, kn))
    sp = jnp.where(causal, sp, 0.0)
    hi, lo = _split_hi_lo(sp)
    later = _mm(hi, upper) + _mm(lo, upper)
    a = jnp.where(causal, jnp.exp(lsig - later), 0.0)
    acc = _mm(a.astype(BF16), vn)
    carry = jnp.sum(sp, axis=-1, keepdims=True)
    for j in range(n_pages - 1, -1, -1):
        pg = sb_pages[j]
        sp, lsig = _softplus_parts(_nt(qbd, pg[:, 0:SB_WIDTH].astype(BF16)))
        hi, lo = _split_hi_lo(sp)
        later = _mm(hi, upper) + _mm(lo, upper) + carry
        acc = acc + _mm(jnp.exp(lsig - later).astype(BF16), pg[:, SB_WIDTH:].astype(BF16))
        carry = carry + jnp.sum(sp, axis=-1, keepdims=True)
    o = jnp.zeros((tn, SB_WIDTH), F32)
    for hd in range(nh):
        o = o + jnp.where((lane_w >> 6) == hd, acc[hd * tn:(hd + 1) * tn], 0.0)
    os_ref[...] = o


def _sample_attn_call(page_table, q, gn, nsa_new, win_new, sbq, sb_new, wc_t, state_win, cache_nsa,
                      cache_sb, layer):
    nseq, tn, _ = q.shape
    n_pages = page_table.shape[1]
    page = cache_nsa.shape[2]
    nwin = state_win.shape[2]
    assert (n_pages * page) % SLC_BLOCK == 0 and tn == 8 and nwin == WINDOW and SLC_BLOCK == 64
    assert (n_pages * page) // SLC_BLOCK - 2 >= N_SEL - 3
    tok = lambda w: pl.BlockSpec((None, tn, w), lambda b, pt: (b, 0, 0))

    def page_spec(w, j):
        return pl.BlockSpec((None, None, page, w), lambda b, pt, j=j: (layer, pt[b, j], 0, 0))

    in_specs = ([tok(512), tok(128), tok(512), tok(256), tok(512), tok(1024),
                 pl.BlockSpec((None, 2, CMP_BLOCK, LANES), lambda b, pt: (layer, 0, 0, 0)),
                 pl.BlockSpec((None, None, nwin, 256), lambda b, pt: (layer, b, 0, 0))]
                + [page_spec(512, j) for j in range(n_pages)]
                + [page_spec(1024, j) for j in range(n_pages)])
    out_shape = [jax.ShapeDtypeStruct((nseq, tn, 512), F32), jax.ShapeDtypeStruct((nseq, tn, 512), F32),
                 jax.ShapeDtypeStruct((nseq, nwin, 256), F32)]
    out_specs = [tok(512), tok(512), pl.BlockSpec((None, nwin, 256), lambda b, pt: (b, 0, 0))]
    grid_spec = pltpu.PrefetchScalarGridSpec(
        num_scalar_prefetch=1, grid=(nseq,), in_specs=in_specs, out_specs=out_specs,
        scratch_shapes=[pltpu.VMEM((LANES, LANES), F32), pltpu.VMEM((LANES, LANES), F32)])
    return pl.pallas_call(
        functools.partial(_sample_attn_body, n_pages=n_pages, page=page),
        out_shape=out_shape,
        grid_spec=grid_spec,
        compiler_params=_cparams(("parallel",)),
        name="sample_attn",
    )(page_table, q, gn, nsa_new, win_new, sbq, sb_new, wc_t, state_win,
      *([cache_nsa] * n_pages), *([cache_sb] * n_pages))


def _post_body(x_ref, gt_ref, on_ref, os_ref, gm_ref, wa_ref, wb_ref, wo_ref, o_ref):
    ya = _mm(on_ref[...].astype(BF16), wa_ref[...])
    yb = _mm(os_ref[...].astype(BF16), wb_ref[...])
    m = gm_ref[:, 0:D_MODEL] * ya + gm_ref[:, D_MODEL:] * yb
    o_ref[...] = x_ref[...] + gt_ref[...] * _mm(m.astype(BF16), wo_ref[...])


def _post_call(x, gt, on, osb, gm, wa, wb, wo, layer, *, tm):
    bsz, t, _ = x.shape
    mrows = gt.shape[1]
    mblk = 1 if mrows == 1 else tm
    mod_map = (lambda b, i: (b, 0, 0)) if mrows == 1 else (lambda b, i: (b, i, 0))
    tok = lambda w: pl.BlockSpec((None, tm, w), lambda b, i: (b, i, 0))
    wspec = lambda k: pl.BlockSpec((None, k, D_MODEL), lambda b, i: (layer, 0, 0))
    return pl.pallas_call(
        _post_body,
        out_shape=jax.ShapeDtypeStruct(x.shape, F32),
        grid=(bsz, t // tm),
        in_specs=[tok(D_MODEL), pl.BlockSpec((None, mblk, D_MODEL), mod_map), tok(512), tok(512),
                  tok(2048), wspec(512), wspec(512), wspec(D_MODEL)],
        out_specs=tok(D_MODEL),
        compiler_params=_cparams(("parallel", "parallel")),
        name="post_attn",
    )(x, gt, on, osb, gm, wa, wb, wo)


def _moe_body(x_ref, sc_ref, sh_ref, gt_ref, g_ref, wr_ref, br_ref, w1_ref, w3_ref, w2_ref, o_ref,
              h_s, gate_s, acc_s):
    e = pl.program_id(2)
    lane = lax.broadcasted_iota(jnp.int32, (1, LANES), 1)
    lanef = lane.astype(F32)

    @pl.when(e == 0)
    def _():
        x = x_ref[...]
        ms = jnp.mean(x * x, axis=-1, keepdims=True)
        h = (x * lax.rsqrt(ms + EPS)) * g_ref[...]
        h = h * (1.0 + sc_ref[...]) + sh_ref[...]
        hb = h.astype(BF16)
        h_s[...] = hb
        logit = _mm(hb, wr_ref[...]) + br_ref[...]
        is_g = lane < N_GROUPS
        lg = jnp.where(is_g, logit, -jnp.inf)
        gmax = jnp.max(lg, axis=-1, keepdims=True)
        grp = jnp.min(jnp.where(lg == gmax, lanef, float(LANES)), axis=-1, keepdims=True)
        p_grp = 1.0 / jnp.sum(jnp.exp(lg - gmax), axis=-1, keepdims=True)
        lo_l = N_GROUPS + grp * EXPERTS_PER_GROUP
        in_grp = (lanef >= lo_l) & (lanef < lo_l + EXPERTS_PER_GROUP)
        le = jnp.where(in_grp, logit, -jnp.inf)
        v1 = jnp.max(le, axis=-1, keepdims=True)
        i1 = jnp.min(jnp.where(le == v1, lanef, float(LANES)), axis=-1, keepdims=True)
        le2 = jnp.where(lanef == i1, -jnp.inf, le)
        v2 = jnp.max(le2, axis=-1, keepdims=True)
        i2 = jnp.min(jnp.where(le2 == v2, lanef, float(LANES)), axis=-1, keepdims=True)
        e2 = jnp.exp(v2 - v1)
        w1 = p_grp * (1.0 / (1.0 + e2))
        w2 = p_grp * (e2 / (1.0 + e2))
        gate_s[...] = jnp.where(lanef == i1, w1, jnp.where(lanef == i2, w2, 0.0))
        acc_s[...] = jnp.zeros(acc_s.shape, F32)

    hb = h_s[...]
    ge = jnp.sum(jnp.where(lane == N_GROUPS + e, gate_s[...], 0.0), axis=-1, keepdims=True)
    u = _mm(hb, w1_ref[...])
    hid = (u * jax.nn.sigmoid(u)) * _mm(hb, w3_ref[...]) * ge
    acc_s[...] += _mm(hid.astype(BF16), w2_ref[...])

    @pl.when(e == pl.num_programs(2) - 1)
    def _():
        o_ref[...] = x_ref[...] + gt_ref[...] * acc_s[...]


def _moe_call(x, sc, sh, gt, g_norm, wr, br, w1, w3, w2, layer, *, tm):
    bsz, t, _ = x.shape
    mrows = sc.shape[1]
    mblk = 1 if mrows == 1 else tm
    mod_map = (lambda b, i, e: (b, 0, 0)) if mrows == 1 else (lambda b, i, e: (b, i, 0))
    tok = pl.BlockSpec((None, tm, D_MODEL), lambda b, i, e: (b, i, 0))
    mod = pl.BlockSpec((None, mblk, D_MODEL), mod_map)
    return pl.pallas_call(
        _moe_body,
        out_shape=jax.ShapeDtypeStruct(x.shape, F32),
        grid=(bsz, t // tm, N_EXPERTS),
        in_specs=[tok, mod, mod, mod,
                  pl.BlockSpec((None, None, 1, D_MODEL), lambda b, i, e: (layer, 1, 0, 0)),
                  pl.BlockSpec((None, D_MODEL, LANES), lambda b, i, e: (layer, 0, 0)),
                  pl.BlockSpec((None, 1, LANES), lambda b, i, e: (layer, 0, 0)),
                  pl.BlockSpec((None, None, D_MODEL, D_EXPERT), lambda b, i, e: (layer, e, 0, 0)),
                  pl.BlockSpec((None, None, D_MODEL, D_EXPERT), lambda b, i, e: (layer, e, 0, 0)),
                  pl.BlockSpec((None, None, D_EXPERT, D_MODEL), lambda b, i, e: (layer, e, 0, 0))],
        out_specs=tok,
        scratch_shapes=[pltpu.VMEM((tm, D_MODEL), BF16), pltpu.VMEM((tm, LANES), F32),
                        pltpu.VMEM((tm, D_MODEL), F32)],
        compiler_params=_cparams(("parallel", "parallel", "arbitrary")),
        name="moe",
    )(x, sc, sh, gt, g_norm, wr, br, w1, w3, w2)


def _rope_tables(pos):
    inv_freq = ROPE_THETA ** (-jnp.arange(0, HEAD_DIM, 2, dtype=F32) / HEAD_DIM)
    ang = pos.astype(F32)[:, None] * inv_freq[None, :]
    cos = jnp.cos(ang)
    sin = jnp.sin(ang)
    cos_t = jnp.concatenate([cos, cos, cos, cos], axis=1)
    sin_t = jnp.concatenate([-sin, sin, -sin, sin], axis=1)
    return cos_t, sin_t


def _even_odd(x):
    return jnp.concatenate([x[:, 0::2], x[:, 1::2]], axis=1)


def _tile(n, pref):
    return pref if n % pref == 0 else n


def _forward(x_prompt, x_sample, cache_nsa, cache_sb, state_win, page_table, c_prompt, c_sample,
             w_ada, b_ada, g_norm, w_in, w_qk_norm, w_cmp, w_br_a, w_br_b, w_out,
             w_route_group, b_route_group, w_route_expert, b_route_expert,
             w_expert_gate, w_expert_up, w_expert_down):
    depth = w_in.shape[0]
    bp, seq, _ = x_prompt.shape
    nseq, tn, _ = x_sample.shape
    n_phys, page = cache_nsa.shape[1], cache_nsa.shape[2]
    past_len = page_table.shape[1] * page

    pad_gn = jnp.zeros((depth, D_MODEL, LANES - 3 * N_HEADS_NSA), F32)
    w_in_p = jnp.concatenate(
        [w_in[:, :, _QPERM], w_in[:, :, 512:1280], w_in[:, :, 1280:1304], pad_gn, w_in[:, :, 1304:]],
        axis=-1).astype(BF16)
    qkn_t = jnp.concatenate([w_qk_norm, w_qk_norm], axis=-1)
    wc_t = jnp.concatenate([w_cmp, w_cmp], axis=-1)
    wa_p = w_br_a[:, _QPERM, :].astype(BF16)
    wb_p = w_br_b.astype(BF16)
    wo_p = w_out.astype(BF16)
    pad_r = jnp.zeros((depth, D_MODEL, LANES - N_GROUPS - N_EXPERTS), F32)
    wr_p = jnp.concatenate([w_route_group, w_route_expert, pad_r], axis=-1).astype(BF16)
    br_p = jnp.concatenate([b_route_group, b_route_expert,
                            jnp.zeros((depth, LANES - N_GROUPS - N_EXPERTS), F32)], axis=-1)
    br_p = br_p.reshape(depth, 1, LANES)
    w1_p = w_expert_gate.astype(BF16)
    w3_p = w_expert_up.astype(BF16)
    w2_p = w_expert_down.astype(BF16)
    g_norm4 = g_norm.reshape(depth, 2, 1, D_MODEL)
    r = lax.broadcasted_iota(jnp.int32, (2 * LANES, LANES), 0)
    c = lax.broadcasted_iota(jnp.int32, (2 * LANES, LANES), 1)
    bd = jnp.where((r % LANES) // HEAD_DIM == c // HEAD_DIM, 1.0, 0.0).astype(BF16)

    cos_p, sin_p = _rope_tables(jnp.arange(seq))
    cos_s, sin_s = _rope_tables(past_len + (jnp.arange(nseq * tn) % tn))

    cache_nsa_v = cache_nsa.reshape(depth, n_phys, page, 4 * N_KV_NSA * HEAD_DIM)
    cache_sb_v = cache_sb.reshape(depth, n_phys, page, 2 * N_HEADS_SB * HEAD_DIM)
    state_win_v = state_win.reshape(depth, nseq, state_win.shape[2], 2 * N_KV_NSA * HEAD_DIM)

    n_c = bp + nseq
    n_cp = -(-n_c // 8) * 8
    c_all = jnp.concatenate([c_prompt, c_sample, jnp.zeros((n_cp - n_c, D_MODEL), F32)], axis=0)
    mod = _ada_call(c_all, w_ada, b_ada)

    def mods(l):
        mp = mod[l, 0:bp].reshape(bp, 1, 6, D_MODEL)
        ms_ = jnp.broadcast_to(mod[l, bp:n_c].reshape(nseq, 1, 6, D_MODEL), (nseq, tn, 6, D_MODEL))
        ms_ = ms_.reshape(1, nseq * tn, 6, D_MODEL)
        return [mp[:, :, k] for k in range(6)], [ms_[:, :, k] for k in range(6)]

    xp = x_prompt
    xs = x_sample.reshape(1, nseq * tn, D_MODEL)
    tm_p = _tile(seq, 256)
    tm_s = _tile(nseq * tn, 256)
    outs = [[] for _ in range(6)]
    for l in range(depth):
        (sh1, sc1, gt1, sh2, sc2, gt2), (sh1s, sc1s, gt1s, sh2s, sc2s, gt2s) = mods(l)

        (q, nsa, nsab, win, winb, gn, sbq, sbkv, sbkvb, gm, kc, vc) = _inproj_call(
            xp, sc1, sh1, g_norm4, w_in_p, qkn_t, cos_p, sin_p, bd, wc_t, l, tm=tm_p, with_cmp=True,
            qdtype=BF16)
        o_n = _nsa_prompt_call(q, gn, _even_odd(kc).astype(BF16), _even_odd(vc).astype(BF16), nsab, winb)
        o_s = _sb_prompt_call(sbq, sbkvb)
        xp = _post_call(xp, gt1, o_n, o_s, gm, wa_p, wb_p, wo_p, l, tm=_tile(seq, 512))
        xp = _moe_call(xp, sc2, sh2, gt2, g_norm4, wr_p, br_p, w1_p, w3_p, w2_p, l, tm=_tile(seq, 1024))
        outs[0].append(nsa.reshape(bp, seq, 4, N_KV_NSA, HEAD_DIM))
        outs[1].append(sbkv.reshape(bp, seq, 2, N_HEADS_SB, HEAD_DIM))
        nw = min(WINDOW, seq)
        outs[2].append(win[:, seq - nw:].reshape(bp, nw, 2, N_KV_NSA, HEAD_DIM))

        (qs_, nsas, _, wins, _, gns, sbqs, sbkvs, _, gms) = _inproj_call(
            xs, sc1s, sh1s, g_norm4, w_in_p, qkn_t, cos_s, sin_s, bd, wc_t, l, tm=tm_s, with_cmp=False,
            qdtype=F32)
        sq = lambda a: a.reshape(nseq, tn, a.shape[-1])
        o_ns, o_ss, win_new = _sample_attn_call(
            page_table, sq(qs_), sq(gns), sq(nsas), sq(wins), sq(sbqs), sq(sbkvs), wc_t, state_win_v,
            cache_nsa_v, cache_sb_v, l)
        xs = _post_call(xs, gt1s, o_ns.reshape(1, nseq * tn, 512), o_ss.reshape(1, nseq * tn, 512), gms,
                        wa_p, wb_p, wo_p, l, tm=tm_s)
        xs = _moe_call(xs, sc2s, sh2s, gt2s, g_norm4, wr_p, br_p, w1_p, w3_p, w2_p, l, tm=tm_s)
        outs[3].append(nsas.reshape(nseq, tn, 4, N_KV_NSA, HEAD_DIM))
        outs[4].append(sbkvs.reshape(nseq, tn, 2, N_HEADS_SB, HEAD_DIM))
        outs[5].append(win_new.reshape(nseq, win_new.shape[1], 2, N_KV_NSA, HEAD_DIM))

    return (xp, xs.reshape(nseq, tn, D_MODEL)) + tuple(jnp.stack(o) for o in outs)


def kernel(x_prompt, x_sample, cache_nsa, cache_sb, state_win, page_table, c_prompt, c_sample, w_ada, b_ada, g_norm, w_in, w_qk_norm, w_cmp, w_br_a, w_br_b, w_out, w_route_group, b_route_group, w_route_expert, b_route_expert, w_expert_gate, w_expert_up, w_expert_down):
    return _forward(x_prompt, x_sample, cache_nsa, cache_sb, state_win, page_table, c_prompt, c_sample,
                    w_ada, b_ada, g_norm, w_in, w_qk_norm, w_cmp, w_br_a, w_br_b, w_out,
                    w_route_group, b_route_group, w_route_expert, b_route_expert,
                    w_expert_gate, w_expert_up, w_expert_down)
```

```python
import functools

import numpy as np
import jax
import jax.numpy as jnp
from jax import lax
from jax.experimental import pallas as pl
from jax.experimental.pallas import tpu as pltpu

F32 = jnp.float32
BF16 = jnp.bfloat16

D_MODEL = 1024
HEAD_DIM = 64
N_HEADS_NSA = 8
N_KV_NSA = 2
GROUP_SIZE = 4
N_HEADS_SB = 8
NSA_WIDTH = 512
SB_WIDTH = 512
CMP_BLOCK = 32
SLC_BLOCK = 64
N_SEL = 16
WINDOW = 512
ROPE_THETA = 10000.0
N_GROUPS = 4
EXPERTS_PER_GROUP = 4
N_EXPERTS = 16
D_EXPERT = 256
EPS = 1e-6
FORCE_SCORE = 1e4
NEG_INF = -1e30
SCALE = HEAD_DIM ** -0.5

LANES = 128
C_Q = 0
C_KV = 512
C_GN = 1280
C_SB = 1408
C_GM = 2944
C_END = 4992
VMEM_LIMIT = 56 * 1024 * 1024

_QPERM = np.concatenate([np.arange(h * 64, (h + 1) * 64) for s in range(4) for h in (s, 4 + s)])


def _nt(a, b):
    return lax.dot_general(a, b, (((1,), (1,)), ((), ())), preferred_element_type=F32)


def _mm(a, b):
    return jnp.dot(a, b, preferred_element_type=F32)


def _split_hi_lo(x):
    hi = x.astype(BF16)
    lo = (x - hi.astype(F32)).astype(BF16)
    return hi, lo


def _cparams(sem):
    return pltpu.CompilerParams(dimension_semantics=sem, vmem_limit_bytes=VMEM_LIMIT)


def _ada_body(c_ref, w_ref, b_ref, o_ref):
    c = c_ref[...]
    a = (c * jax.nn.sigmoid(c)).astype(BF16)
    o_ref[...] = _mm(a, w_ref[...].astype(BF16)) + b_ref[...]


def _ada_call(c_all, w_ada, b_ada):
    depth = w_ada.shape[0]
    n = c_all.shape[0]
    tn = 1536
    return pl.pallas_call(
        _ada_body,
        out_shape=jax.ShapeDtypeStruct((depth, n, 6 * D_MODEL), F32),
        grid=(depth, 6 * D_MODEL // tn),
        in_specs=[pl.BlockSpec((n, D_MODEL), lambda l, j: (0, 0)),
                  pl.BlockSpec((None, D_MODEL, tn), lambda l, j: (l, 0, j)),
                  pl.BlockSpec((None, 1, tn), lambda l, j: (l, 0, j))],
        out_specs=pl.BlockSpec((None, n, tn), lambda l, j: (l, 0, j)),
        compiler_params=_cparams(("parallel", "parallel")),
        name="ada_mod",
    )(c_all, w_ada, b_ada.reshape(depth, 1, 6 * D_MODEL))


def _inproj_body(x_ref, sc_ref, sh_ref, g_ref, w_ref, qkn_ref, cos_ref, sin_ref, bd_ref, wc_ref,
                 *outs, with_cmp):
    if with_cmp:
        (q_ref, nsa_ref, nsab_ref, win_ref, winb_ref, gn_ref, sbq_ref, sbkv_ref, sbkvb_ref,
         gm_ref, kc_ref, vc_ref) = outs
    else:
        (q_ref, nsa_ref, nsab_ref, win_ref, winb_ref, gn_ref, sbq_ref, sbkv_ref, sbkvb_ref,
         gm_ref) = outs
    x = x_ref[...]
    tm = x.shape[0]
    ms = jnp.mean(x * x, axis=-1, keepdims=True)
    h = (x * lax.rsqrt(ms + EPS)) * g_ref[...]
    h = h * (1.0 + sc_ref[...]) + sh_ref[...]
    hb = h.astype(BF16)
    cos = cos_ref[...]
    sin = sin_ref[...]
    bd = bd_ref[...]
    lane = lax.broadcasted_iota(jnp.int32, (1, LANES), 1)
    first_half = (lane & (HEAD_DIM - 1)) < HEAD_DIM // 2

    def seg(lo, hi):
        return _mm(hb, w_ref[:, lo:hi])

    def norm_rope(xs, gq):
        hi, lo = _split_hi_lo(xs * xs)
        ssum = _mm(jnp.concatenate([hi, lo], axis=1), bd)
        y = xs * lax.rsqrt(ssum * (1.0 / HEAD_DIM) + EPS) * gq
        sw = jnp.where(first_half, pltpu.roll(y, 96, 1), pltpu.roll(y, 32, 1))
        return y * cos + sw * sin

    pq = seg(C_Q, C_KV)
    for s in range(4):
        qs = norm_rope(pq[:, s * LANES:(s + 1) * LANES], qkn_ref[0:1, :]) * SCALE
        q_ref[:, s * LANES:(s + 1) * LANES] = qs.astype(q_ref.dtype)

    pkv = seg(C_KV, C_GN)
    kc = norm_rope(pkv[:, 0:128], qkn_ref[1:2, :])
    vc = pkv[:, 128:256]
    ks = norm_rope(pkv[:, 256:384], qkn_ref[2:3, :])
    vs = pkv[:, 384:512]
    kw = norm_rope(pkv[:, 512:640], qkn_ref[3:4, :])
    vw = pkv[:, 640:768]
    for j, v in enumerate((kc, vc, ks, vs)):
        nsa_ref[:, j * LANES:(j + 1) * LANES] = v
        nsab_ref[:, j * LANES:(j + 1) * LANES] = v.astype(BF16)
    for j, v in enumerate((kw, vw)):
        win_ref[:, j * LANES:(j + 1) * LANES] = v
        winb_ref[:, j * LANES:(j + 1) * LANES] = v.astype(BF16)
    if with_cmp:
        nb = tm // CMP_BLOCK
        kc_ref[...] = jnp.sum(kc.reshape(nb, CMP_BLOCK, LANES) * wc_ref[0][None], axis=1)
        vc_ref[...] = jnp.sum(vc.reshape(nb, CMP_BLOCK, LANES) * wc_ref[1][None], axis=1)

    gn_ref[...] = jax.nn.sigmoid(seg(C_GN, C_SB))

    psb = seg(C_SB, C_GM)
    sbq_ref[...] = (psb[:, 0:SB_WIDTH] * SCALE).astype(sbq_ref.dtype)
    sbkv_ref[...] = psb[:, SB_WIDTH:]
    sbkvb_ref[...] = psb[:, SB_WIDTH:].astype(BF16)

    gm_ref[...] = jax.nn.sigmoid(seg(C_GM, C_END))


def _inproj_call(x, sc, sh, g_norm, w_in_p, qkn_t, cos_t, sin_t, bd, wc_t, layer, *, tm, with_cmp,
                 qdtype):
    bsz, t, _ = x.shape
    mrows = sc.shape[1]
    mblk = 1 if mrows == 1 else tm
    mod_map = (lambda b, i: (b, 0, 0)) if mrows == 1 else (lambda b, i: (b, i, 0))
    tok = lambda w: pl.BlockSpec((None, tm, w), lambda b, i: (b, i, 0))
    in_specs = [
        tok(D_MODEL),
        pl.BlockSpec((None, mblk, D_MODEL), mod_map),
        pl.BlockSpec((None, mblk, D_MODEL), mod_map),
        pl.BlockSpec((None, None, 1, D_MODEL), lambda b, i: (layer, 0, 0, 0)),
        pl.BlockSpec((None, D_MODEL, C_END), lambda b, i: (layer, 0, 0)),
        pl.BlockSpec((None, 4, LANES), lambda b, i: (layer, 0, 0)),
        pl.BlockSpec((tm, LANES), lambda b, i: (i, 0)),
        pl.BlockSpec((tm, LANES), lambda b, i: (i, 0)),
        pl.BlockSpec((2 * LANES, LANES), lambda b, i: (0, 0)),
        pl.BlockSpec((None, 2, CMP_BLOCK, LANES), lambda b, i: (layer, 0, 0, 0)),
    ]
    sds = lambda w, dt: jax.ShapeDtypeStruct((bsz, t, w), dt)
    out_shape = [sds(512, qdtype), sds(512, F32), sds(512, BF16), sds(256, F32), sds(256, BF16),
                 sds(128, F32), sds(512, qdtype), sds(1024, F32), sds(1024, BF16), sds(2048, F32)]
    out_specs = [tok(512), tok(512), tok(512), tok(256), tok(256), tok(128), tok(512), tok(1024),
                 tok(1024), tok(2048)]
    if with_cmp:
        nb = tm // CMP_BLOCK
        out_shape += [jax.ShapeDtypeStruct((bsz, t // CMP_BLOCK, LANES), F32)] * 2
        out_specs += [pl.BlockSpec((None, nb, LANES), lambda b, i: (b, i, 0))] * 2
    return pl.pallas_call(
        functools.partial(_inproj_body, with_cmp=with_cmp),
        out_shape=out_shape,
        grid=(bsz, t // tm),
        in_specs=in_specs,
        out_specs=out_specs,
        compiler_params=_cparams(("parallel", "parallel")),
        name="inproj",
    )(x, sc, sh, g_norm, w_in_p, qkn_t, cos_t, sin_t, bd, wc_t)


TQ = 128
TKS = 512


def _masked_softmax(s, mask):
    sm = jnp.where(mask, s, NEG_INF)
    mx = jnp.max(sm, axis=-1, keepdims=True)
    e = jnp.exp(sm - mx)
    den = jnp.sum(e, axis=-1, keepdims=True)
    return jnp.where(mask, e / den, 0.0)


def _top_sel_rows(score, n_pick):
    nbk = score.shape[0]
    jrow = lax.broadcasted_iota(jnp.int32, score.shape, 0).astype(F32)
    sel = jnp.zeros(score.shape, F32)
    for _ in range(n_pick):
        m = jnp.max(score, axis=0, keepdims=True)
        idx = jnp.min(jnp.where(score == m, jrow, float(nbk)), axis=0, keepdims=True)
        pick = jrow == idx
        sel = jnp.where(pick, 1.0, sel)
        score = jnp.where(pick, -jnp.inf, score)
    return sel


def _nsa_prompt_body(q_ref, gn_ref, kc_ref, vc_ref, ks_ref, vs_ref, kw_ref, vw_ref, o_ref,
                     s_buf, m_ref, l_ref, acc_ref, oc_ref, sel_ref):
    i = pl.program_id(1)
    q0 = i * TQ
    ncmp = kc_ref.shape[0]
    nslc = ncmp // 2
    lane = lax.broadcasted_iota(jnp.int32, (1, LANES), 1)
    gates = gn_ref[...]
    posq = q0 + lax.broadcasted_iota(jnp.int32, (TQ, 1), 0)
    posq4 = q0 + (lax.broadcasted_iota(jnp.int32, (GROUP_SIZE * TQ, 1), 0) & (TQ - 1))

    lc = lax.broadcasted_iota(jnp.int32, (1, ncmp), 1)
    blk = jnp.where(lc < nslc, 2 * lc, 2 * (lc - nslc) + 1)
    mc4 = ((blk + 1) * CMP_BLOCK - 1) <= posq4

    jrow = lax.broadcasted_iota(jnp.int32, (nslc, TQ), 0)
    cur = (q0 + lax.broadcasted_iota(jnp.int32, (nslc, TQ), 1)) >> 6
    forced = (jrow == 0) | (jrow == cur) | (jrow == cur - 1)
    valid = jrow <= cur

    wlen = WINDOW + TQ
    wstart = pl.multiple_of(jnp.maximum(q0 - WINDOW, 0), TQ)
    posw = wstart + lax.broadcasted_iota(jnp.int32, (1, wlen), 1)
    dist = posq4 - posw
    mw4 = (dist >= 0) & (dist < WINDOW)

    n_tiles = (q0 + TQ + TKS - 1) // TKS
    lmasks = [lane < HEAD_DIM, lane >= HEAD_DIM]
    zero = jnp.zeros((), q_ref.dtype)

    def q_group(g):
        return jnp.concatenate(
            [jnp.where(lmasks[g], q_ref[:, s * LANES:(s + 1) * LANES], zero) for s in range(4)], axis=0)

    scores = []
    for g in range(N_KV_NSA):
        pc = _masked_softmax(_nt(q_group(g), kc_ref[...]), mc4)
        oc_ref[g] = _mm(pc.astype(BF16), vc_ref[...])
        imp = pc[0:TQ] + pc[TQ:2 * TQ] + pc[2 * TQ:3 * TQ] + pc[3 * TQ:4 * TQ]
        imp = imp[:, 0:nslc] + imp[:, nslc:ncmp]
        if nslc < LANES:
            imp = jnp.concatenate([imp, jnp.zeros((TQ, LANES - nslc), F32)], axis=1)
        imp_t = imp.T[0:nslc, :]
        scores.append(jnp.where(forced, FORCE_SCORE, jnp.where(valid, imp_t, -1.0)))
    sel_t = _top_sel_rows(jnp.concatenate(scores, axis=1), min(N_SEL, nslc))
    for g in range(N_KV_NSA):
        st = sel_t[:, g * TQ:(g + 1) * TQ]
        if nslc < LANES:
            st = jnp.concatenate([st, jnp.zeros((LANES - nslc, TQ), F32)], axis=0)
        sel_ref[g] = st.T.astype(BF16)

    s_buf[...] = jnp.full(s_buf.shape, NEG_INF, F32)
    m_ref[...] = jnp.full(m_ref.shape, NEG_INF, F32)
    l_ref[...] = jnp.zeros(l_ref.shape, F32)
    acc_ref[...] = jnp.zeros(acc_ref.shape, F32)

    def sel_trip(it, c):
        kb0 = pl.multiple_of(jnp.maximum(it - 1, 0) * TKS, TKS)
        ka0 = pl.multiple_of(jnp.minimum(it, n_tiles - 1) * TKS, TKS)
        kpos = ka0 + lax.broadcasted_iota(jnp.int32, (LANES, TKS), 1)
        brow = lax.broadcasted_iota(jnp.int32, (LANES, TKS), 0)
        expand = jnp.where(brow == (kpos >> 6), 1.0, 0.0).astype(BF16)
        causal = kpos[0:1, :] <= posq
        for g in range(N_KV_NSA):
            sa = _nt(q_group(g), ks_ref[pl.ds(ka0, TKS), :])
            picked = _mm(sel_ref[g], expand)
            bias = jnp.where((picked > 0.5) & causal, 0.0, NEG_INF)

            s = s_buf[g]
            m_old = m_ref[g]
            m_new = jnp.maximum(m_old, jnp.max(s, axis=-1, keepdims=True))
            alpha = jnp.exp(m_old - m_new)
            p = jnp.exp(s - m_new)
            l_ref[g] = alpha * l_ref[g] + jnp.sum(p, axis=-1, keepdims=True)
            m_ref[g] = m_new
            acc_ref[g] = alpha * acc_ref[g] + _mm(p.astype(BF16), vs_ref[pl.ds(kb0, TKS), :])
            s_buf[g] = jnp.concatenate(
                [sa[r * TQ:(r + 1) * TQ] + bias for r in range(GROUP_SIZE)], axis=0)
        return c

    lax.fori_loop(0, n_tiles + 1, sel_trip, 0)

    out = [jnp.zeros((TQ, LANES), F32) for _ in range(4)]
    for g in range(N_KV_NSA):
        osel = acc_ref[g] / l_ref[g]
        oc = oc_ref[g]
        pw = _masked_softmax(_nt(q_group(g), kw_ref[pl.ds(wstart, wlen), :]), mw4)
        ow = _mm(pw.astype(BF16), vw_ref[pl.ds(wstart, wlen), :])
        for s in range(4):
            hd = s + 4 * g
            r = slice(s * TQ, (s + 1) * TQ)
            comb = (gates[:, 3 * hd:3 * hd + 1] * oc[r] + gates[:, 3 * hd + 1:3 * hd + 2] * osel[r]
                    + gates[:, 3 * hd + 2:3 * hd + 3] * ow[r])
            out[s] = jnp.where(lmasks[g], comb, out[s])
    for s in range(4):
        o_ref[:, s * LANES:(s + 1) * LANES] = out[s].astype(o_ref.dtype)


def _nsa_prompt_call(q, gn, kcp, vcp, nsab, winb):
    bsz, t, _ = q.shape
    ncmp = kcp.shape[1]
    rows = GROUP_SIZE * TQ
    full = lambda col: pl.BlockSpec((None, t, LANES), lambda b, i, c=col: (b, 0, c))
    return pl.pallas_call(
        _nsa_prompt_body,
        out_shape=jax.ShapeDtypeStruct((bsz, t, NSA_WIDTH), BF16),
        grid=(bsz, t // TQ),
        in_specs=[pl.BlockSpec((None, TQ, NSA_WIDTH), lambda b, i: (b, i, 0)),
                  pl.BlockSpec((None, TQ, LANES), lambda b, i: (b, i, 0)),
                  pl.BlockSpec((None, ncmp, LANES), lambda b, i: (b, 0, 0)),
                  pl.BlockSpec((None, ncmp, LANES), lambda b, i: (b, 0, 0)),
                  full(2), full(3), full(0), full(1)],
        out_specs=pl.BlockSpec((None, TQ, NSA_WIDTH), lambda b, i: (b, i, 0)),
        scratch_shapes=[pltpu.VMEM((N_KV_NSA, rows, TKS), F32), pltpu.VMEM((N_KV_NSA, rows, 1), F32),
                        pltpu.VMEM((N_KV_NSA, rows, 1), F32), pltpu.VMEM((N_KV_NSA, rows, LANES), F32),
                        pltpu.VMEM((N_KV_NSA, rows, LANES), F32), pltpu.VMEM((N_KV_NSA, TQ, LANES), BF16)],
        compiler_params=_cparams(("parallel", "parallel")),
        name="nsa_prompt",
    )(q, gn, kcp, vcp, nsab, nsab, winb, winb)


TKB = 256


def _softplus_parts(z):
    sp = jnp.maximum(z, 0.0) + jnp.log(1.0 + jnp.exp(-jnp.abs(z)))
    return sp, z - sp


def _suffix_matrix(n):
    r = lax.broadcasted_iota(jnp.int32, (n, n), 0)
    c = lax.broadcasted_iota(jnp.int32, (n, n), 1)
    return jnp.where(r > c, 1.0, 0.0).astype(BF16)


SB_SLABS = 2
SB_UNDERFLOW = 100.0


def _sb_prompt_body(q_ref, k_ref, v_ref, o_ref, z_buf, e_buf, g_buf, sp_buf, acc_ref):
    i = pl.program_id(2)
    q0 = i * TQ
    lane = lax.broadcasted_iota(jnp.int32, (1, LANES), 1)
    zero = jnp.zeros((), q_ref.dtype)
    posq2 = q0 + (lax.broadcasted_iota(jnp.int32, (2 * TQ, 1), 0) & (TQ - 1))
    upper = _suffix_matrix(TKB)
    t_last = (q0 + TQ - 1) // TKB

    def q_stack(c):
        q2 = q_ref[:, c * LANES:(c + 1) * LANES]
        return jnp.concatenate([jnp.where(lane < HEAD_DIM, q2, zero),
                                jnp.where(lane >= HEAD_DIM, q2, zero)], axis=0)

    def k_tile(c, t):
        return k_ref[pl.ds(pl.multiple_of(t * TKB, TKB), TKB), c * LANES:(c + 1) * LANES]

    def v_tile(c, t):
        return v_ref[pl.ds(pl.multiple_of(t * TKB, TKB), TKB), c * LANES:(c + 1) * LANES]

    causal = (t_last * TKB + lax.broadcasted_iota(jnp.int32, (1, TKB), 1)) < posq2
    run0 = []
    for c in range(SB_SLABS):
        sp, lsig = _softplus_parts(_nt(q_stack(c), k_tile(c, t_last)))
        sp = jnp.where(causal, sp, 0.0)
        a = jnp.where(causal, jnp.exp(lsig - _mm(sp.astype(BF16), upper)), 0.0)
        acc_ref[c] = _mm(a.astype(BF16), v_tile(c, t_last))
        run0.append(jnp.sum(sp, axis=-1, keepdims=True))

    n_rest = t_last
    z_buf[...] = jnp.zeros(z_buf.shape, F32)
    e_buf[...] = jnp.zeros(e_buf.shape, F32)
    g_buf[...] = jnp.zeros(g_buf.shape, F32)
    sp_buf[...] = jnp.zeros(sp_buf.shape, BF16)

    def body(carry):
        it, limit, runs = carry
        ta = jnp.clip(t_last - 1 - it, 0, t_last)
        jc = it - 3
        wc = jnp.where((jc >= 0) & (jc < n_rest), 1.0, 0.0)
        tc = jnp.clip(t_last - 1 - jc, 0, t_last)
        jb = it - 1
        wb = jnp.where((jb >= 0) & (jb < n_rest), 1.0, 0.0)
        new_runs = []
        for c in range(SB_SLABS):
            z_new = _nt(q_stack(c), k_tile(c, ta))
            acc_ref[c] += wc * _mm(jnp.exp(e_buf[c]).astype(BF16), v_tile(c, tc))
            e_buf[c] = g_buf[c] - _mm(sp_buf[c], upper)
            sp, lsig = _softplus_parts(z_buf[c])
            sp_buf[c] = sp.astype(BF16)
            g_buf[c] = lsig - runs[c]
            z_buf[c] = z_new
            new_runs.append(runs[c] + wb * jnp.sum(sp, axis=-1, keepdims=True))
        low = functools.reduce(jnp.minimum, [jnp.min(r) for r in new_runs])
        limit = jnp.where(low >= SB_UNDERFLOW, jnp.minimum(limit, it + 3), limit)
        return it + 1, limit, tuple(new_runs)

    lax.while_loop(lambda carry: carry[0] < carry[1], body,
                   (jnp.int32(0), jnp.where(n_rest > 0, n_rest + 3, 0), tuple(run0)))
    for c in range(SB_SLABS):
        acc = acc_ref[c]
        o_ref[:, c * LANES:(c + 1) * LANES] = jnp.where(
            lane < HEAD_DIM, acc[0:TQ], acc[TQ:2 * TQ]).astype(o_ref.dtype)


def _sb_prompt_call(sbq, sbkvb):
    bsz, t, _ = sbq.shape
    w = SB_SLABS * LANES
    ngrp = SB_WIDTH // w
    buf = lambda last, dt: pltpu.VMEM((SB_SLABS, 2 * TQ, last), dt)
    return pl.pallas_call(
        _sb_prompt_body,
        out_shape=jax.ShapeDtypeStruct((bsz, t, SB_WIDTH), BF16),
        grid=(bsz, ngrp, t // TQ),
        in_specs=[pl.BlockSpec((None, TQ, w), lambda b, s, i: (b, i, s)),
                  pl.BlockSpec((None, t, w), lambda b, s, i: (b, 0, s)),
                  pl.BlockSpec((None, t, w), lambda b, s, i: (b, 0, ngrp + s))],
        out_specs=pl.BlockSpec((None, TQ, w), lambda b, s, i: (b, i, s)),
        scratch_shapes=[buf(TKB, F32), buf(TKB, F32), buf(TKB, F32), buf(TKB, BF16), buf(LANES, F32)],
        compiler_params=_cparams(("parallel", "parallel", "parallel")),
        name="sb_prompt",
    )(sbq, sbkvb, sbkvb)


def _pad_rows(x, n):
    return jnp.concatenate([x, jnp.zeros((n - x.shape[0], x.shape[1]), x.dtype)], axis=0)


def _sample_attn_body(pt_ref, q_ref, gn_ref, nsan_ref, winn_ref, sbq_ref, sbn_ref, wc_ref, win_ref,
                      *rest, n_pages, page):
    nsa_pages = rest[:n_pages]
    sb_pages = rest[n_pages:2 * n_pages]
    on_ref, os_ref, wout_ref, kc_s, vc_s = rest[2 * n_pages:]
    del pt_ref
    tn = q_ref.shape[0]
    nrow = N_KV_NSA * GROUP_SIZE * tn
    n_past = n_pages * page
    per_page = page // CMP_BLOCK
    ncmp = n_pages * per_page
    nslc_past = n_past // SLC_BLOCK
    lane = lax.broadcasted_iota(jnp.int32, (1, LANES), 1)
    row = lax.broadcasted_iota(jnp.int32, (nrow, 1), 0)
    tq_row = row & (tn - 1)

    qs = q_ref[...].astype(F32)
    qb = jnp.concatenate(
        [jnp.where((lane < HEAD_DIM) if g == 0 else (lane >= HEAD_DIM), qs[:, s * LANES:(s + 1) * LANES], 0.0)
         for g in range(N_KV_NSA) for s in range(4)], axis=0).astype(BF16)

    kc_s[...] = jnp.zeros(kc_s.shape, F32)
    vc_s[...] = jnp.zeros(vc_s.shape, F32)
    half = ncmp // 2
    for j in range(n_pages):
        pg = nsa_pages[j]
        kcj = jnp.sum(pg[:, 0:128].reshape(per_page, CMP_BLOCK, LANES) * wc_ref[0][None], axis=1)
        vcj = jnp.sum(pg[:, 128:256].reshape(per_page, CMP_BLOCK, LANES) * wc_ref[1][None], axis=1)
        for r in range(per_page):
            n = j * per_page + r
            dst = n // 2 + (half if n % 2 else 0)
            kc_s[dst:dst + 1, :] = kcj[r:r + 1, :]
            vc_s[dst:dst + 1, :] = vcj[r:r + 1, :]
    mc = lane < ncmp
    pc = _masked_softmax(_nt(qb, kc_s[...].astype(BF16)), mc)
    oc = _mm(pc.astype(BF16), vc_s[...].astype(BF16))
    gq = GROUP_SIZE * tn
    imp = jnp.concatenate(
        [sum(pc[g * gq + s * tn:g * gq + (s + 1) * tn] for s in range(4)) for g in range(N_KV_NSA)],
        axis=0)
    imp = imp + pltpu.roll(imp, LANES - half, 1)
    n_free = N_SEL - 3
    cand = (lane >= 1) & (lane <= nslc_past - 2)
    score = jnp.where(cand, imp, -jnp.inf)
    rank = jnp.zeros(score.shape, F32)
    for r in range(1, nslc_past):
        rank = rank + jnp.where(pltpu.roll(score, r, 1) >= score, 1.0, 0.0)
        rank = rank + jnp.where(pltpu.roll(score, LANES - r, 1) > score, 1.0, 0.0)
    sel = (cand & (rank < n_free)) | (lane == 0) | (lane == nslc_past - 1)
    sel = jnp.where(sel, 1.0, 0.0)
    selrows = jnp.concatenate([sel[g * tn:(g + 1) * tn] for g in range(N_KV_NSA) for _ in range(4)],
                              axis=0).astype(BF16)

    brow = lax.broadcasted_iota(jnp.int32, (LANES, page), 0)
    kcol = lax.broadcasted_iota(jnp.int32, (LANES, page), 1)
    new_ok = (lane <= tq_row) & (lane < tn)
    s_tiles = []
    for j in range(n_pages):
        s = _nt(qb, nsa_pages[j][:, 256:384].astype(BF16))
        expand = jnp.where(brow == ((j * page + kcol) >> 6), 1.0, 0.0).astype(BF16)
        s_tiles.append(jnp.where(_mm(selrows, expand) > 0.5, s, NEG_INF))
    ksn = _pad_rows(nsan_ref[:, 256:384], LANES).astype(BF16)
    vsn = _pad_rows(nsan_ref[:, 384:512], LANES).astype(BF16)
    s_tiles.append(jnp.where(new_ok, _nt(qb, ksn), NEG_INF))
    mx = functools.reduce(jnp.maximum, [jnp.max(s, axis=-1, keepdims=True) for s in s_tiles])
    p_tiles = [jnp.exp(s - mx) for s in s_tiles]
    den = sum(jnp.sum(p, axis=-1, keepdims=True) for p in p_tiles)
    osel = _mm(p_tiles[n_pages].astype(BF16), vsn)
    for j in range(n_pages):
        osel = osel + _mm(p_tiles[j].astype(BF16), nsa_pages[j][:, 384:512].astype(BF16))
    osel = osel / den

    nwin = win_ref.shape[0]
    wcol = lax.broadcasted_iota(jnp.int32, (1, nwin), 1)
    dist = (nwin + tq_row) - wcol
    mwin = (dist >= 0) & (dist < WINDOW)
    s_w = jnp.where(mwin, _nt(qb, win_ref[:, 0:128].astype(BF16)), NEG_INF)
    kwn = _pad_rows(winn_ref[:, 0:128], LANES).astype(BF16)
    vwn = _pad_rows(winn_ref[:, 128:256], LANES).astype(BF16)
    s_wn = jnp.where(new_ok, _nt(qb, kwn), NEG_INF)
    mxw = jnp.maximum(jnp.max(s_w, axis=-1, keepdims=True), jnp.max(s_wn, axis=-1, keepdims=True))
    p_w = jnp.exp(s_w - mxw)
    p_wn = jnp.exp(s_wn - mxw)
    denw = jnp.sum(p_w, axis=-1, keepdims=True) + jnp.sum(p_wn, axis=-1, keepdims=True)
    ow = (_mm(p_w.astype(BF16), win_ref[:, 128:256].astype(BF16)) + _mm(p_wn.astype(BF16), vwn)) / denw

    gates = gn_ref[...]
    for s in range(4):
        parts = []
        for g in range(N_KV_NSA):
            hd = s + 4 * g
            r = slice(g * gq + s * tn, g * gq + (s + 1) * tn)
            parts.append(gates[:, 3 * hd:3 * hd + 1] * oc[r] + gates[:, 3 * hd + 1:3 * hd + 2] * osel[r]
                         + gates[:, 3 * hd + 2:3 * hd + 3] * ow[r])
        on_ref[:, s * LANES:(s + 1) * LANES] = jnp.where(lane < HEAD_DIM, parts[0], parts[1])

    wout_ref[0:nwin - tn, :] = win_ref[tn:nwin, :]
    wout_ref[nwin - tn:nwin, :] = winn_ref[...]

    nh = N_HEADS_SB
    hrow = lax.broadcasted_iota(jnp.int32, (nh * tn, 1), 0)
    lane_w = lax.broadcasted_iota(jnp.int32, (1, SB_WIDTH), 1)
    sq = sbq_ref[...].astype(F32)
    qbd = jnp.where((lane_w >> 6) == (hrow >> 3), jnp.concatenate([sq] * nh, axis=0), 0.0).astype(BF16)
    tq_h = hrow & (tn - 1)
    upper = _suffix_matrix(page)
    kn = _pad_rows(sbn_ref[:, 0:SB_WIDTH], page).astype(BF16)
    vn = _pad_rows(sbn_ref[:, SB_WIDTH:], page).astype(BF16)
    lane_p = lax.broadcasted_iota(jnp.int32, (1, page), 1)
    causal = lane_p < tq_h
    sp, lsig = _softplus_parts(_nt(qbd, kn))
    sp = jnp.where(causal, sp, 0.0)
    hi, lo = _split_hi_lo(sp)
    later = _mm(hi, upper) + _mm(lo, upper)
    a = jnp.where(causal, jnp.exp(lsig - later), 0.0)
    acc = _mm(a.astype(BF16), vn)
    carry = jnp.sum(sp, axis=-1, keepdims=True)
    for j in range(n_pages - 1, -1, -1):
        pg = sb_pages[j]
        sp, lsig = _softplus_parts(_nt(qbd, pg[:, 0:SB_WIDTH].astype(BF16)))
        hi, lo = _split_hi_lo(sp)
        later = _mm(hi, upper) + _mm(lo, upper) + carry
        acc = acc + _mm(jnp.exp(lsig - later).astype(BF16), pg[:, SB_WIDTH:].astype(BF16))
        carry = carry + jnp.sum(sp, axis=-1, keepdims=True)
    o = jnp.zeros((tn, SB_WIDTH), F32)
    for hd in range(nh):
        o = o + jnp.where((lane_w >> 6) == hd, acc[hd * tn:(hd + 1) * tn], 0.0)
    os_ref[...] = o


def _sample_attn_call(page_table, q, gn, nsa_new, win_new, sbq, sb_new, wc_t, state_win, cache_nsa,
                      cache_sb, layer):
    nseq, tn, _ = q.shape
    n_pages = page_table.shape[1]
    page = cache_nsa.shape[2]
    nwin = state_win.shape[2]
    assert (n_pages * page) % SLC_BLOCK == 0 and tn == 8 and nwin == WINDOW and SLC_BLOCK == 64
    assert (n_pages * page) // SLC_BLOCK - 2 >= N_SEL - 3
    tok = lambda w: pl.BlockSpec((None, tn, w), lambda b, pt: (b, 0, 0))

    def page_spec(w, j):
        return pl.BlockSpec((None, None, page, w), lambda b, pt, j=j: (layer, pt[b, j], 0, 0))

    in_specs = ([tok(512), tok(128), tok(512), tok(256), tok(512), tok(1024),
                 pl.BlockSpec((None, 2, CMP_BLOCK, LANES), lambda b, pt: (layer, 0, 0, 0)),
                 pl.BlockSpec((None, None, nwin, 256), lambda b, pt: (layer, b, 0, 0))]
                + [page_spec(512, j) for j in range(n_pages)]
                + [page_spec(1024, j) for j in range(n_pages)])
    out_shape = [jax.ShapeDtypeStruct((nseq, tn, 512), F32), jax.ShapeDtypeStruct((nseq, tn, 512), F32),
                 jax.ShapeDtypeStruct((nseq, nwin, 256), F32)]
    out_specs = [tok(512), tok(512), pl.BlockSpec((None, nwin, 256), lambda b, pt: (b, 0, 0))]
    grid_spec = pltpu.PrefetchScalarGridSpec(
        num_scalar_prefetch=1, grid=(nseq,), in_specs=in_specs, out_specs=out_specs,
        scratch_shapes=[pltpu.VMEM((LANES, LANES), F32), pltpu.VMEM((LANES, LANES), F32)])
    return pl.pallas_call(
        functools.partial(_sample_attn_body, n_pages=n_pages, page=page),
        out_shape=out_shape,
        grid_spec=grid_spec,
        compiler_params=_cparams(("parallel",)),
        name="sample_attn",
    )(page_table, q, gn, nsa_new, win_new, sbq, sb_new, wc_t, state_win,
      *([cache_nsa] * n_pages), *([cache_sb] * n_pages))


def _post_body(x_ref, gt_ref, on_ref, os_ref, gm_ref, wa_ref, wb_ref, wo_ref, o_ref):
    ya = _mm(on_ref[...].astype(BF16), wa_ref[...])
    yb = _mm(os_ref[...].astype(BF16), wb_ref[...])
    m = gm_ref[:, 0:D_MODEL] * ya + gm_ref[:, D_MODEL:] * yb
    o_ref[...] = x_ref[...] + gt_ref[...] * _mm(m.astype(BF16), wo_ref[...])


def _post_call(x, gt, on, osb, gm, wa, wb, wo, layer, *, tm):
    bsz, t, _ = x.shape
    mrows = gt.shape[1]
    mblk = 1 if mrows == 1 else tm
    mod_map = (lambda b, i: (b, 0, 0)) if mrows == 1 else (lambda b, i: (b, i, 0))
    tok = lambda w: pl.BlockSpec((None, tm, w), lambda b, i: (b, i, 0))
    wspec = lambda k: pl.BlockSpec((None, k, D_MODEL), lambda b, i: (layer, 0, 0))
    return pl.pallas_call(
        _post_body,
        out_shape=jax.ShapeDtypeStruct(x.shape, F32),
        grid=(bsz, t // tm),
        in_specs=[tok(D_MODEL), pl.BlockSpec((None, mblk, D_MODEL), mod_map), tok(512), tok(512),
                  tok(2048), wspec(512), wspec(512), wspec(D_MODEL)],
        out_specs=tok(D_MODEL),
        compiler_params=_cparams(("parallel", "parallel")),
        name="post_attn",
    )(x, gt, on, osb, gm, wa, wb, wo)


def _moe_body(x_ref, sc_ref, sh_ref, gt_ref, g_ref, wr_ref, br_ref, w1_ref, w3_ref, w2_ref, o_ref,
              h_s, gate_s, acc_s):
    e = pl.program_id(2)
    lane = lax.broadcasted_iota(jnp.int32, (1, LANES), 1)
    lanef = lane.astype(F32)

    @pl.when(e == 0)
    def _():
        x = x_ref[...]
        ms = jnp.mean(x * x, axis=-1, keepdims=True)
        h = (x * lax.rsqrt(ms + EPS)) * g_ref[...]
        h = h * (1.0 + sc_ref[...]) + sh_ref[...]
        hb = h.astype(BF16)
        h_s[...] = hb
        logit = _mm(hb, wr_ref[...]) + br_ref[...]
        is_g = lane < N_GROUPS
        lg = jnp.where(is_g, logit, -jnp.inf)
        gmax = jnp.max(lg, axis=-1, keepdims=True)
        grp = jnp.min(jnp.where(lg == gmax, lanef, float(LANES)), axis=-1, keepdims=True)
        p_grp = 1.0 / jnp.sum(jnp.exp(lg - gmax), axis=-1, keepdims=True)
        lo_l = N_GROUPS + grp * EXPERTS_PER_GROUP
        in_grp = (lanef >= lo_l) & (lanef < lo_l + EXPERTS_PER_GROUP)
        le = jnp.where(in_grp, logit, -jnp.inf)
        v1 = jnp.max(le, axis=-1, keepdims=True)
        i1 = jnp.min(jnp.where(le == v1, lanef, float(LANES)), axis=-1, keepdims=True)
        le2 = jnp.where(lanef == i1, -jnp.inf, le)
        v2 = jnp.max(le2, axis=-1, keepdims=True)
        i2 = jnp.min(jnp.where(le2 == v2, lanef, float(LANES)), axis=-1, keepdims=True)
        e2 = jnp.exp(v2 - v1)
        w1 = p_grp * (1.0 / (1.0 + e2))
        w2 = p_grp * (e2 / (1.0 + e2))
        gate_s[...] = jnp.where(lanef == i1, w1, jnp.where(lanef == i2, w2, 0.0))
        acc_s[...] = jnp.zeros(acc_s.shape, F32)

    hb = h_s[...]
    ge = jnp.sum(jnp.where(lane == N_GROUPS + e, gate_s[...], 0.0), axis=-1, keepdims=True)
    u = _mm(hb, w1_ref[...])
    hid = (u * jax.nn.sigmoid(u)) * _mm(hb, w3_ref[...]) * ge
    acc_s[...] += _mm(hid.astype(BF16), w2_ref[...])

    @pl.when(e == pl.num_programs(2) - 1)
    def _():
        o_ref[...] = x_ref[...] + gt_ref[...] * acc_s[...]


def _moe_call(x, sc, sh, gt, g_norm, wr, br, w1, w3, w2, layer, *, tm):
    bsz, t, _ = x.shape
    mrows = sc.shape[1]
    mblk = 1 if mrows == 1 else tm
    mod_map = (lambda b, i, e: (b, 0, 0)) if mrows == 1 else (lambda b, i, e: (b, i, 0))
    tok = pl.BlockSpec((None, tm, D_MODEL), lambda b, i, e: (b, i, 0))
    mod = pl.BlockSpec((None, mblk, D_MODEL), mod_map)
    return pl.pallas_call(
        _moe_body,
        out_shape=jax.ShapeDtypeStruct(x.shape, F32),
        grid=(bsz, t // tm, N_EXPERTS),
        in_specs=[tok, mod, mod, mod,
                  pl.BlockSpec((None, None, 1, D_MODEL), lambda b, i, e: (layer, 1, 0, 0)),
                  pl.BlockSpec((None, D_MODEL, LANES), lambda b, i, e: (layer, 0, 0)),
                  pl.BlockSpec((None, 1, LANES), lambda b, i, e: (layer, 0, 0)),
                  pl.BlockSpec((None, None, D_MODEL, D_EXPERT), lambda b, i, e: (layer, e, 0, 0)),
                  pl.BlockSpec((None, None, D_MODEL, D_EXPERT), lambda b, i, e: (layer, e, 0, 0)),
                  pl.BlockSpec((None, None, D_EXPERT, D_MODEL), lambda b, i, e: (layer, e, 0, 0))],
        out_specs=tok,
        scratch_shapes=[pltpu.VMEM((tm, D_MODEL), BF16), pltpu.VMEM((tm, LANES), F32),
                        pltpu.VMEM((tm, D_MODEL), F32)],
        compiler_params=_cparams(("parallel", "parallel", "arbitrary")),
        name="moe",
    )(x, sc, sh, gt, g_norm, wr, br, w1, w3, w2)


def _rope_tables(pos):
    inv_freq = ROPE_THETA ** (-jnp.arange(0, HEAD_DIM, 2, dtype=F32) / HEAD_DIM)
    ang = pos.astype(F32)[:, None] * inv_freq[None, :]
    cos = jnp.cos(ang)
    sin = jnp.sin(ang)
    cos_t = jnp.concatenate([cos, cos, cos, cos], axis=1)
    sin_t = jnp.concatenate([-sin, sin, -sin, sin], axis=1)
    return cos_t, sin_t


def _even_odd(x):
    return jnp.concatenate([x[:, 0::2], x[:, 1::2]], axis=1)


def _tile(n, pref):
    return pref if n % pref == 0 else n


def _forward(x_prompt, x_sample, cache_nsa, cache_sb, state_win, page_table, c_prompt, c_sample,
             w_ada, b_ada, g_norm, w_in, w_qk_norm, w_cmp, w_br_a, w_br_b, w_out,
             w_route_group, b_route_group, w_route_expert, b_route_expert,
             w_expert_gate, w_expert_up, w_expert_down):
    depth = w_in.shape[0]
    bp, seq, _ = x_prompt.shape
    nseq, tn, _ = x_sample.shape
    n_phys, page = cache_nsa.shape[1], cache_nsa.shape[2]
    past_len = page_table.shape[1] * page

    pad_gn = jnp.zeros((depth, D_MODEL, LANES - 3 * N_HEADS_NSA), F32)
    w_in_p = jnp.concatenate(
        [w_in[:, :, _QPERM], w_in[:, :, 512:1280], w_in[:, :, 1280:1304], pad_gn, w_in[:, :, 1304:]],
        axis=-1).astype(BF16)
    qkn_t = jnp.concatenate([w_qk_norm, w_qk_norm], axis=-1)
    wc_t = jnp.concatenate([w_cmp, w_cmp], axis=-1)
    wa_p = w_br_a[:, _QPERM, :].astype(BF16)
    wb_p = w_br_b.astype(BF16)
    wo_p = w_out.astype(BF16)
    pad_r = jnp.zeros((depth, D_MODEL, LANES - N_GROUPS - N_EXPERTS), F32)
    wr_p = jnp.concatenate([w_route_group, w_route_expert, pad_r], axis=-1).astype(BF16)
    br_p = jnp.concatenate([b_route_group, b_route_expert,
                            jnp.zeros((depth, LANES - N_GROUPS - N_EXPERTS), F32)], axis=-1)
    br_p = br_p.reshape(depth, 1, LANES)
    w1_p = w_expert_gate.astype(BF16)
    w3_p = w_expert_up.astype(BF16)
    w2_p = w_expert_down.astype(BF16)
    g_norm4 = g_norm.reshape(depth, 2, 1, D_MODEL)
    r = lax.broadcasted_iota(jnp.int32, (2 * LANES, LANES), 0)
    c = lax.broadcasted_iota(jnp.int32, (2 * LANES, LANES), 1)
    bd = jnp.where((r % LANES) // HEAD_DIM == c // HEAD_DIM, 1.0, 0.0).astype(BF16)

    cos_p, sin_p = _rope_tables(jnp.arange(seq))
    cos_s, sin_s = _rope_tables(past_len + (jnp.arange(nseq * tn) % tn))

    cache_nsa_v = cache_nsa.reshape(depth, n_phys, page, 4 * N_KV_NSA * HEAD_DIM)
    cache_sb_v = cache_sb.reshape(depth, n_phys, page, 2 * N_HEADS_SB * HEAD_DIM)
    state_win_v = state_win.reshape(depth, nseq, state_win.shape[2], 2 * N_KV_NSA * HEAD_DIM)

    n_c = bp + nseq
    n_cp = -(-n_c // 8) * 8
    c_all = jnp.concatenate([c_prompt, c_sample, jnp.zeros((n_cp - n_c, D_MODEL), F32)], axis=0)
    mod = _ada_call(c_all, w_ada, b_ada)

    def mods(l):
        mp = mod[l, 0:bp].reshape(bp, 1, 6, D_MODEL)
        ms_ = jnp.broadcast_to(mod[l, bp:n_c].reshape(nseq, 1, 6, D_MODEL), (nseq, tn, 6, D_MODEL))
        ms_ = ms_.reshape(1, nseq * tn, 6, D_MODEL)
        return [mp[:, :, k] for k in range(6)], [ms_[:, :, k] for k in range(6)]

    xp = x_prompt
    xs = x_sample.reshape(1, nseq * tn, D_MODEL)
    tm_p = _tile(seq, 256)
    tm_s = _tile(nseq * tn, 256)
    outs = [[] for _ in range(6)]
    for l in range(depth):
        (sh1, sc1, gt1, sh2, sc2, gt2), (sh1s, sc1s, gt1s, sh2s, sc2s, gt2s) = mods(l)

        (q, nsa, nsab, win, winb, gn, sbq, sbkv, sbkvb, gm, kc, vc) = _inproj_call(
            xp, sc1, sh1, g_norm4, w_in_p, qkn_t, cos_p, sin_p, bd, wc_t, l, tm=tm_p, with_cmp=True,
            qdtype=BF16)
        o_n = _nsa_prompt_call(q, gn, _even_odd(kc).astype(BF16), _even_odd(vc).astype(BF16), nsab, winb)
        o_s = _sb_prompt_call(sbq, sbkvb)
        xp = _post_call(xp, gt1, o_n, o_s, gm, wa_p, wb_p, wo_p, l, tm=_tile(seq, 512))
        xp = _moe_call(xp, sc2, sh2, gt2, g_norm4, wr_p, br_p, w1_p, w3_p, w2_p, l, tm=_tile(seq, 1024))
        outs[0].append(nsa.reshape(bp, seq, 4, N_KV_NSA, HEAD_DIM))
        outs[1].append(sbkv.reshape(bp, seq, 2, N_HEADS_SB, HEAD_DIM))
        nw = min(WINDOW, seq)
        outs[2].append(win[:, seq - nw:].reshape(bp, nw, 2, N_KV_NSA, HEAD_DIM))

        (qs_, nsas, _, wins, _, gns, sbqs, sbkvs, _, gms) = _inproj_call(
            xs, sc1s, sh1s, g_norm4, w_in_p, qkn_t, cos_s, sin_s, bd, wc_t, l, tm=tm_s, with_cmp=False,
            qdtype=F32)
        sq = lambda a: a.reshape(nseq, tn, a.shape[-1])
        o_ns, o_ss, win_new = _sample_attn_call(
            page_table, sq(qs_), sq(gns), sq(nsas), sq(wins), sq(sbqs), sq(sbkvs), wc_t, state_win_v,
            cache_nsa_v, cache_sb_v, l)
        xs = _post_call(xs, gt1s, o_ns.reshape(1, nseq * tn, 512), o_ss.reshape(1, nseq * tn, 512), gms,
                        wa_p, wb_p, wo_p, l, tm=tm_s)
        xs = _moe_call(xs, sc2s, sh2s, gt2s, g_norm4, wr_p, br_p, w1_p, w3_p, w2_p, l, tm=tm_s)
        outs[3].append(nsas.reshape(nseq, tn, 4, N_KV_NSA, HEAD_DIM))
        outs[4].append(sbkvs.reshape(nseq, tn, 2, N_HEADS_SB, HEAD_DIM))
        outs[5].append(win_new.reshape(nseq, win_new.shape[1], 2, N_KV_NSA, HEAD_DIM))

    return (xp, xs.reshape(nseq, tn, D_MODEL)) + tuple(jnp.stack(o) for o in outs)


def kernel(x_prompt, x_sample, cache_nsa, cache_sb, state_win, page_table, c_prompt, c_sample, w_ada, b_ada, g_norm, w_in, w_qk_norm, w_cmp, w_br_a, w_br_b, w_out, w_route_group, b_route_group, w_route_expert, b_route_expert, w_expert_gate, w_expert_up, w_expert_down):
    return _forward(x_prompt, x_sample, cache_nsa, cache_sb, state_win, page_table, c_prompt, c_sample,
                    w_ada, b_ada, g_norm, w_in, w_qk_norm, w_cmp, w_br_a, w_br_b, w_out,
                    w_route_group, b_route_group, w_route_expert, b_route_expert,
                    w_expert_gate, w_expert_up, w_expert_down)
```

```python
import functools

import numpy as np
import jax
import jax.numpy as jnp
from jax import lax
from jax.experimental import pallas as pl
from jax.experimental.pallas import tpu as pltpu

F32 = jnp.float32
BF16 = jnp.bfloat16

D_MODEL = 1024
HEAD_DIM = 64
N_HEADS_NSA = 8
N_KV_NSA = 2
GROUP_SIZE = 4
N_HEADS_SB = 8
NSA_WIDTH = 512
SB_WIDTH = 512
CMP_BLOCK = 32
SLC_BLOCK = 64
N_SEL = 16
WINDOW = 512
ROPE_THETA = 10000.0
N_GROUPS = 4
EXPERTS_PER_GROUP = 4
N_EXPERTS = 16
D_EXPERT = 256
EPS = 1e-6
FORCE_SCORE = 1e4
NEG_INF = -1e30
SCALE = HEAD_DIM ** -0.5

LANES = 128
C_Q = 0
C_KV = 512
C_GN = 1280
C_SB = 1408
C_GM = 2944
C_END = 4992
VMEM_LIMIT = 56 * 1024 * 1024

_QPERM = np.concatenate([np.arange(h * 64, (h + 1) * 64) for s in range(4) for h in (s, 4 + s)])


def _nt(a, b):
    return lax.dot_general(a, b, (((1,), (1,)), ((), ())), preferred_element_type=F32)


def _mm(a, b):
    return jnp.dot(a, b, preferred_element_type=F32)


def _split_hi_lo(x):
    hi = x.astype(BF16)
    lo = (x - hi.astype(F32)).astype(BF16)
    return hi, lo


def _cparams(sem):
    return pltpu.CompilerParams(dimension_semantics=sem, vmem_limit_bytes=VMEM_LIMIT)


def _ada_body(c_ref, w_ref, b_ref, o_ref):
    c = c_ref[...]
    a = (c * jax.nn.sigmoid(c)).astype(BF16)
    o_ref[...] = _mm(a, w_ref[...].astype(BF16)) + b_ref[...]


def _ada_call(c_all, w_ada, b_ada):
    depth = w_ada.shape[0]
    n = c_all.shape[0]
    tn = 1536
    return pl.pallas_call(
        _ada_body,
        out_shape=jax.ShapeDtypeStruct((depth, n, 6 * D_MODEL), F32),
        grid=(depth, 6 * D_MODEL // tn),
        in_specs=[pl.BlockSpec((n, D_MODEL), lambda l, j: (0, 0)),
                  pl.BlockSpec((None, D_MODEL, tn), lambda l, j: (l, 0, j)),
                  pl.BlockSpec((None, 1, tn), lambda l, j: (l, 0, j))],
        out_specs=pl.BlockSpec((None, n, tn), lambda l, j: (l, 0, j)),
        compiler_params=_cparams(("parallel", "parallel")),
        name="ada_mod",
    )(c_all, w_ada, b_ada.reshape(depth, 1, 6 * D_MODEL))


def _inproj_body(x_ref, sc_ref, sh_ref, g_ref, w_ref, qkn_ref, cos_ref, sin_ref, bd_ref, wc_ref,
                 *outs, with_cmp):
    if with_cmp:
        (q_ref, nsa_ref, nsab_ref, win_ref, winb_ref, gn_ref, sbq_ref, sbkv_ref, sbkvb_ref,
         gm_ref, kc_ref, vc_ref) = outs
    else:
        (q_ref, nsa_ref, nsab_ref, win_ref, winb_ref, gn_ref, sbq_ref, sbkv_ref, sbkvb_ref,
         gm_ref) = outs
    x = x_ref[...]
    tm = x.shape[0]
    ms = jnp.mean(x * x, axis=-1, keepdims=True)
    h = (x * lax.rsqrt(ms + EPS)) * g_ref[...]
    h = h * (1.0 + sc_ref[...]) + sh_ref[...]
    hb = h.astype(BF16)
    cos = cos_ref[...]
    sin = sin_ref[...]
    bd = bd_ref[...]
    lane = lax.broadcasted_iota(jnp.int32, (1, LANES), 1)
    first_half = (lane & (HEAD_DIM - 1)) < HEAD_DIM // 2

    def seg(lo, hi):
        return _mm(hb, w_ref[:, lo:hi])

    def norm_rope(xs, gq):
        hi, lo = _split_hi_lo(xs * xs)
        ssum = _mm(jnp.concatenate([hi, lo], axis=1), bd)
        y = xs * lax.rsqrt(ssum * (1.0 / HEAD_DIM) + EPS) * gq
        sw = jnp.where(first_half, pltpu.roll(y, 96, 1), pltpu.roll(y, 32, 1))
        return y * cos + sw * sin

    pq = seg(C_Q, C_KV)
    for s in range(4):
        qs = norm_rope(pq[:, s * LANES:(s + 1) * LANES], qkn_ref[0:1, :]) * SCALE
        q_ref[:, s * LANES:(s + 1) * LANES] = qs.astype(q_ref.dtype)

    pkv = seg(C_KV, C_GN)
    kc = norm_rope(pkv[:, 0:128], qkn_ref[1:2, :])
    vc = pkv[:, 128:256]
    ks = norm_rope(pkv[:, 256:384], qkn_ref[2:3, :])
    vs = pkv[:, 384:512]
    kw = norm_rope(pkv[:, 512:640], qkn_ref[3:4, :])
    vw = pkv[:, 640:768]
    for j, v in enumerate((kc, vc, ks, vs)):
        nsa_ref[:, j * LANES:(j + 1) * LANES] = v
        nsab_ref[:, j * LANES:(j + 1) * LANES] = v.astype(BF16)
    for j, v in enumerate((kw, vw)):
        win_ref[:, j * LANES:(j + 1) * LANES] = v
        winb_ref[:, j * LANES:(j + 1) * LANES] = v.astype(BF16)
    if with_cmp:
        nb = tm // CMP_BLOCK
        kc_ref[...] = jnp.sum(kc.reshape(nb, CMP_BLOCK, LANES) * wc_ref[0][None], axis=1)
        vc_ref[...] = jnp.sum(vc.reshape(nb, CMP_BLOCK, LANES) * wc_ref[1][None], axis=1)

    gn_ref[...] = jax.nn.sigmoid(seg(C_GN, C_SB))

    psb = seg(C_SB, C_GM)
    sbq_ref[...] = (psb[:, 0:SB_WIDTH] * SCALE).astype(sbq_ref.dtype)
    sbkv_ref[...] = psb[:, SB_WIDTH:]
    sbkvb_ref[...] = psb[:, SB_WIDTH:].astype(BF16)

    gm_ref[...] = jax.nn.sigmoid(seg(C_GM, C_END))


def _inproj_call(x, sc, sh, g_norm, w_in_p, qkn_t, cos_t, sin_t, bd, wc_t, layer, *, tm, with_cmp,
                 qdtype):
    bsz, t, _ = x.shape
    mrows = sc.shape[1]
    mblk = 1 if mrows == 1 else tm
    mod_map = (lambda b, i: (b, 0, 0)) if mrows == 1 else (lambda b, i: (b, i, 0))
    tok = lambda w: pl.BlockSpec((None, tm, w), lambda b, i: (b, i, 0))
    in_specs = [
        tok(D_MODEL),
        pl.BlockSpec((None, mblk, D_MODEL), mod_map),
        pl.BlockSpec((None, mblk, D_MODEL), mod_map),
        pl.BlockSpec((None, None, 1, D_MODEL), lambda b, i: (layer, 0, 0, 0)),
        pl.BlockSpec((None, D_MODEL, C_END), lambda b, i: (layer, 0, 0)),
        pl.BlockSpec((None, 4, LANES), lambda b, i: (layer, 0, 0)),
        pl.BlockSpec((tm, LANES), lambda b, i: (i, 0)),
        pl.BlockSpec((tm, LANES), lambda b, i: (i, 0)),
        pl.BlockSpec((2 * LANES, LANES), lambda b, i: (0, 0)),
        pl.BlockSpec((None, 2, CMP_BLOCK, LANES), lambda b, i: (layer, 0, 0, 0)),
    ]
    sds = lambda w, dt: jax.ShapeDtypeStruct((bsz, t, w), dt)
    out_shape = [sds(512, qdtype), sds(512, F32), sds(512, BF16), sds(256, F32), sds(256, BF16),
                 sds(128, F32), sds(512, qdtype), sds(1024, F32), sds(1024, BF16), sds(2048, F32)]
    out_specs = [tok(512), tok(512), tok(512), tok(256), tok(256), tok(128), tok(512), tok(1024),
                 tok(1024), tok(2048)]
    if with_cmp:
        nb = tm // CMP_BLOCK
        out_shape += [jax.ShapeDtypeStruct((bsz, t // CMP_BLOCK, LANES), F32)] * 2
        out_specs += [pl.BlockSpec((None, nb, LANES), lambda b, i: (b, i, 0))] * 2
    return pl.pallas_call(
        functools.partial(_inproj_body, with_cmp=with_cmp),
        out_shape=out_shape,
        grid=(bsz, t // tm),
        in_specs=in_specs,
        out_specs=out_specs,
        compiler_params=_cparams(("parallel", "parallel")),
        name="inproj",
    )(x, sc, sh, g_norm, w_in_p, qkn_t, cos_t, sin_t, bd, wc_t)


TQ = 128
TKS = 512


def _masked_softmax(s, mask):
    sm = jnp.where(mask, s, NEG_INF)
    mx = jnp.max(sm, axis=-1, keepdims=True)
    e = jnp.exp(sm - mx)
    den = jnp.sum(e, axis=-1, keepdims=True)
    return jnp.where(mask, e / den, 0.0)


def _top_sel_rows(score, n_pick):
    nbk = score.shape[0]
    jrow = lax.broadcasted_iota(jnp.int32, score.shape, 0).astype(F32)
    sel = jnp.zeros(score.shape, F32)
    for _ in range(n_pick):
        m = jnp.max(score, axis=0, keepdims=True)
        idx = jnp.min(jnp.where(score == m, jrow, float(nbk)), axis=0, keepdims=True)
        pick = jrow == idx
        sel = jnp.where(pick, 1.0, sel)
        score = jnp.where(pick, -jnp.inf, score)
    return sel


def _nsa_prompt_body(q_ref, gn_ref, kc_ref, vc_ref, ks_ref, vs_ref, kw_ref, vw_ref, o_ref,
                     s_buf, m_ref, l_ref, acc_ref, oc_ref, sel_ref):
    i = pl.program_id(1)
    q0 = i * TQ
    ncmp = kc_ref.shape[0]
    nslc = ncmp // 2
    lane = lax.broadcasted_iota(jnp.int32, (1, LANES), 1)
    gates = gn_ref[...]
    posq = q0 + lax.broadcasted_iota(jnp.int32, (TQ, 1), 0)
    posq4 = q0 + (lax.broadcasted_iota(jnp.int32, (GROUP_SIZE * TQ, 1), 0) & (TQ - 1))

    lc = lax.broadcasted_iota(jnp.int32, (1, ncmp), 1)
    blk = jnp.where(lc < nslc, 2 * lc, 2 * (lc - nslc) + 1)
    mc4 = ((blk + 1) * CMP_BLOCK - 1) <= posq4

    jrow = lax.broadcasted_iota(jnp.int32, (nslc, TQ), 0)
    cur = (q0 + lax.broadcasted_iota(jnp.int32, (nslc, TQ), 1)) >> 6
    forced = (jrow == 0) | (jrow == cur) | (jrow == cur - 1)
    valid = jrow <= cur

    wlen = WINDOW + TQ
    wstart = pl.multiple_of(jnp.maximum(q0 - WINDOW, 0), TQ)
    posw = wstart + lax.broadcasted_iota(jnp.int32, (1, wlen), 1)
    dist = posq4 - posw
    mw4 = (dist >= 0) & (dist < WINDOW)

    n_tiles = (q0 + TQ + TKS - 1) // TKS
    lmasks = [lane < HEAD_DIM, lane >= HEAD_DIM]
    zero = jnp.zeros((), q_ref.dtype)

    def q_group(g):
        return jnp.concatenate(
            [jnp.where(lmasks[g], q_ref[:, s * LANES:(s + 1) * LANES], zero) for s in range(4)], axis=0)

    scores = []
    for g in range(N_KV_NSA):
        pc = _masked_softmax(_nt(q_group(g), kc_ref[...]), mc4)
        oc_ref[g] = _mm(pc.astype(BF16), vc_ref[...])
        imp = pc[0:TQ] + pc[TQ:2 * TQ] + pc[2 * TQ:3 * TQ] + pc[3 * TQ:4 * TQ]
        imp = imp[:, 0:nslc] + imp[:, nslc:ncmp]
        if nslc < LANES:
            imp = jnp.concatenate([imp, jnp.zeros((TQ, LANES - nslc), F32)], axis=1)
        imp_t = imp.T[0:nslc, :]
        scores.append(jnp.where(forced, FORCE_SCORE, jnp.where(valid, imp_t, -1.0)))
    sel_t = _top_sel_rows(jnp.concatenate(scores, axis=1), min(N_SEL, nslc))
    for g in range(N_KV_NSA):
        st = sel_t[:, g * TQ:(g + 1) * TQ]
        if nslc < LANES:
            st = jnp.concatenate([st, jnp.zeros((LANES - nslc, TQ), F32)], axis=0)
        sel_ref[g] = st.T.astype(BF16)

    s_buf[...] = jnp.full(s_buf.shape, NEG_INF, F32)
    m_ref[...] = jnp.full(m_ref.shape, NEG_INF, F32)
    l_ref[...] = jnp.zeros(l_ref.shape, F32)
    acc_ref[...] = jnp.zeros(acc_ref.shape, F32)

    def sel_trip(it, c):
        kb0 = pl.multiple_of(jnp.maximum(it - 1, 0) * TKS, TKS)
        ka0 = pl.multiple_of(jnp.minimum(it, n_tiles - 1) * TKS, TKS)
        kpos = ka0 + lax.broadcasted_iota(jnp.int32, (LANES, TKS), 1)
        brow = lax.broadcasted_iota(jnp.int32, (LANES, TKS), 0)
        expand = jnp.where(brow == (kpos >> 6), 1.0, 0.0).astype(BF16)
        causal = kpos[0:1, :] <= posq
        for g in range(N_KV_NSA):
            sa = _nt(q_group(g), ks_ref[pl.ds(ka0, TKS), :])
            picked = _mm(sel_ref[g], expand)
            bias = jnp.where((picked > 0.5) & causal, 0.0, NEG_INF)

            s = s_buf[g]
            m_old = m_ref[g]
            m_new = jnp.maximum(m_old, jnp.max(s, axis=-1, keepdims=True))
            alpha = jnp.exp(m_old - m_new)
            p = jnp.exp(s - m_new)
            l_ref[g] = alpha * l_ref[g] + jnp.sum(p, axis=-1, keepdims=True)
            m_ref[g] = m_new
            acc_ref[g] = alpha * acc_ref[g] + _mm(p.astype(BF16), vs_ref[pl.ds(kb0, TKS), :])
            s_buf[g] = jnp.concatenate(
                [sa[r * TQ:(r + 1) * TQ] + bias for r in range(GROUP_SIZE)], axis=0)
        return c

    lax.fori_loop(0, n_tiles + 1, sel_trip, 0)

    out = [jnp.zeros((TQ, LANES), F32) for _ in range(4)]
    for g in range(N_KV_NSA):
        osel = acc_ref[g] / l_ref[g]
        oc = oc_ref[g]
        pw = _masked_softmax(_nt(q_group(g), kw_ref[pl.ds(wstart, wlen), :]), mw4)
        ow = _mm(pw.astype(BF16), vw_ref[pl.ds(wstart, wlen), :])
        for s in range(4):
            hd = s + 4 * g
            r = slice(s * TQ, (s + 1) * TQ)
            comb = (gates[:, 3 * hd:3 * hd + 1] * oc[r] + gates[:, 3 * hd + 1:3 * hd + 2] * osel[r]
                    + gates[:, 3 * hd + 2:3 * hd + 3] * ow[r])
            out[s] = jnp.where(lmasks[g], comb, out[s])
    for s in range(4):
        o_ref[:, s * LANES:(s + 1) * LANES] = out[s].astype(o_ref.dtype)


def _nsa_prompt_call(q, gn, kcp, vcp, nsab, winb):
    bsz, t, _ = q.shape
    ncmp = kcp.shape[1]
    rows = GROUP_SIZE * TQ
    full = lambda col: pl.BlockSpec((None, t, LANES), lambda b, i, c=col: (b, 0, c))
    return pl.pallas_call(
        _nsa_prompt_body,
        out_shape=jax.ShapeDtypeStruct((bsz, t, NSA_WIDTH), BF16),
        grid=(bsz, t // TQ),
        in_specs=[pl.BlockSpec((None, TQ, NSA_WIDTH), lambda b, i: (b, i, 0)),
                  pl.BlockSpec((None, TQ, LANES), lambda b, i: (b, i, 0)),
                  pl.BlockSpec((None, ncmp, LANES), lambda b, i: (b, 0, 0)),
                  pl.BlockSpec((None, ncmp, LANES), lambda b, i: (b, 0, 0)),
                  full(2), full(3), full(0), full(1)],
        out_specs=pl.BlockSpec((None, TQ, NSA_WIDTH), lambda b, i: (b, i, 0)),
        scratch_shapes=[pltpu.VMEM((N_KV_NSA, rows, TKS), F32), pltpu.VMEM((N_KV_NSA, rows, 1), F32),
                        pltpu.VMEM((N_KV_NSA, rows, 1), F32), pltpu.VMEM((N_KV_NSA, rows, LANES), F32),
                        pltpu.VMEM((N_KV_NSA, rows, LANES), F32), pltpu.VMEM((N_KV_NSA, TQ, LANES), BF16)],
        compiler_params=_cparams(("parallel", "parallel")),
        name="nsa_prompt",
    )(q, gn, kcp, vcp, nsab, nsab, winb, winb)


TKB = 256


def _softplus_parts(z):
    sp = jnp.maximum(z, 0.0) + jnp.log(1.0 + jnp.exp(-jnp.abs(z)))
    return sp, z - sp


def _suffix_matrix(n):
    r = lax.broadcasted_iota(jnp.int32, (n, n), 0)
    c = lax.broadcasted_iota(jnp.int32, (n, n), 1)
    return jnp.where(r > c, 1.0, 0.0).astype(BF16)


SB_SLABS = 2
SB_UNDERFLOW = 100.0


def _sb_prompt_body(q_ref, k_ref, v_ref, o_ref, z_buf, e_buf, g_buf, sp_buf, acc_ref):
    i = pl.program_id(2)
    q0 = i * TQ
    lane = lax.broadcasted_iota(jnp.int32, (1, LANES), 1)
    zero = jnp.zeros((), q_ref.dtype)
    posq2 = q0 + (lax.broadcasted_iota(jnp.int32, (2 * TQ, 1), 0) & (TQ - 1))
    upper = _suffix_matrix(TKB)
    t_last = (q0 + TQ - 1) // TKB

    def q_stack(c):
        q2 = q_ref[:, c * LANES:(c + 1) * LANES]
        return jnp.concatenate([jnp.where(lane < HEAD_DIM, q2, zero),
                                jnp.where(lane >= HEAD_DIM, q2, zero)], axis=0)

    def k_tile(c, t):
        return k_ref[pl.ds(pl.multiple_of(t * TKB, TKB), TKB), c * LANES:(c + 1) * LANES]

    def v_tile(c, t):
        return v_ref[pl.ds(pl.multiple_of(t * TKB, TKB), TKB), c * LANES:(c + 1) * LANES]

    causal = (t_last * TKB + lax.broadcasted_iota(jnp.int32, (1, TKB), 1)) < posq2
    run0 = []
    for c in range(SB_SLABS):
        sp, lsig = _softplus_parts(_nt(q_stack(c), k_tile(c, t_last)))
        sp = jnp.where(causal, sp, 0.0)
        a = jnp.where(causal, jnp.exp(lsig - _mm(sp.astype(BF16), upper)), 0.0)
        acc_ref[c] = _mm(a.astype(BF16), v_tile(c, t_last))
        run0.append(jnp.sum(sp, axis=-1, keepdims=True))

    n_rest = t_last
    z_buf[...] = jnp.zeros(z_buf.shape, F32)
    e_buf[...] = jnp.zeros(e_buf.shape, F32)
    g_buf[...] = jnp.zeros(g_buf.shape, F32)
    sp_buf[...] = jnp.zeros(sp_buf.shape, BF16)

    def body(carry):
        it, limit, runs = carry
        ta = jnp.clip(t_last - 1 - it, 0, t_last)
        jc = it - 3
        wc = jnp.where((jc >= 0) & (jc < n_rest), 1.0, 0.0)
        tc = jnp.clip(t_last - 1 - jc, 0, t_last)
        jb = it - 1
        wb = jnp.where((jb >= 0) & (jb < n_rest), 1.0, 0.0)
        new_runs = []
        for c in range(SB_SLABS):
            z_new = _nt(q_stack(c), k_tile(c, ta))
            acc_ref[c] += wc * _mm(jnp.exp(e_buf[c]).astype(BF16), v_tile(c, tc))
            e_buf[c] = g_buf[c] - _mm(sp_buf[c], upper)
            sp, lsig = _softplus_parts(z_buf[c])
            sp_buf[c] = sp.astype(BF16)
            g_buf[c] = lsig - runs[c]
            z_buf[c] = z_new
            new_runs.append(runs[c] + wb * jnp.sum(sp, axis=-1, keepdims=True))
        low = functools.reduce(jnp.minimum, [jnp.min(r) for r in new_runs])
        limit = jnp.where(low >= SB_UNDERFLOW, jnp.minimum(limit, it + 3), limit)
        return it + 1, limit, tuple(new_runs)

    lax.while_loop(lambda carry: carry[0] < carry[1], body,
                   (jnp.int32(0), jnp.where(n_rest > 0, n_rest + 3, 0), tuple(run0)))
    for c in range(SB_SLABS):
        acc = acc_ref[c]
        o_ref[:, c * LANES:(c + 1) * LANES] = jnp.where(
            lane < HEAD_DIM, acc[0:TQ], acc[TQ:2 * TQ]).astype(o_ref.dtype)


def _sb_prompt_call(sbq, sbkvb):
    bsz, t, _ = sbq.shape
    w = SB_SLABS * LANES
    ngrp = SB_WIDTH // w
    buf = lambda last, dt: pltpu.VMEM((SB_SLABS, 2 * TQ, last), dt)
    return pl.pallas_call(
        _sb_prompt_body,
        out_shape=jax.ShapeDtypeStruct((bsz, t, SB_WIDTH), BF16),
        grid=(bsz, ngrp, t // TQ),
        in_specs=[pl.BlockSpec((None, TQ, w), lambda b, s, i: (b, i, s)),
                  pl.BlockSpec((None, t, w), lambda b, s, i: (b, 0, s)),
                  pl.BlockSpec((None, t, w), lambda b, s, i: (b, 0, ngrp + s))],
        out_specs=pl.BlockSpec((None, TQ, w), lambda b, s, i: (b, i, s)),
        scratch_shapes=[buf(TKB, F32), buf(TKB, F32), buf(TKB, F32), buf(TKB, BF16), buf(LANES, F32)],
        compiler_params=_cparams(("parallel", "parallel", "parallel")),
        name="sb_prompt",
    )(sbq, sbkvb, sbkvb)


def _pad_rows(x, n):
    return jnp.concatenate([x, jnp.zeros((n - x.shape[0], x.shape[1]), x.dtype)], axis=0)


def _sample_nsa_body(pt_ref, q_ref, gn_ref, nsan_ref, winn_ref, wc_ref, win_ref, *rest, n_pages, page):
    nsa_pages = rest[:n_pages]
    on_ref, wout_ref, kc_s, vc_s = rest[n_pages:]
    del pt_ref
    tn = q_ref.shape[0]
    nrow = N_KV_NSA * GROUP_SIZE * tn
    n_past = n_pages * page
    per_page = page // CMP_BLOCK
    ncmp = n_pages * per_page
    nslc_past = n_past // SLC_BLOCK
    lane = lax.broadcasted_iota(jnp.int32, (1, LANES), 1)
    row = lax.broadcasted_iota(jnp.int32, (nrow, 1), 0)
    tq_row = row & (tn - 1)

    qs = q_ref[...].astype(F32)
    qb = jnp.concatenate(
        [jnp.where((lane < HEAD_DIM) if g == 0 else (lane >= HEAD_DIM), qs[:, s * LANES:(s + 1) * LANES], 0.0)
         for g in range(N_KV_NSA) for s in range(4)], axis=0).astype(BF16)

    kc_s[...] = jnp.zeros(kc_s.shape, F32)
    vc_s[...] = jnp.zeros(vc_s.shape, F32)
    half = ncmp // 2
    for j in range(n_pages):
        pg = nsa_pages[j]
        kcj = jnp.sum(pg[:, 0:128].reshape(per_page, CMP_BLOCK, LANES) * wc_ref[0][None], axis=1)
        vcj = jnp.sum(pg[:, 128:256].reshape(per_page, CMP_BLOCK, LANES) * wc_ref[1][None], axis=1)
        for r in range(per_page):
            n = j * per_page + r
            dst = n // 2 + (half if n % 2 else 0)
            kc_s[dst:dst + 1, :] = kcj[r:r + 1, :]
            vc_s[dst:dst + 1, :] = vcj[r:r + 1, :]
    mc = lane < ncmp
    pc = _masked_softmax(_nt(qb, kc_s[...].astype(BF16)), mc)
    oc = _mm(pc.astype(BF16), vc_s[...].astype(BF16))
    gq = GROUP_SIZE * tn
    imp = jnp.concatenate(
        [sum(pc[g * gq + s * tn:g * gq + (s + 1) * tn] for s in range(4)) for g in range(N_KV_NSA)],
        axis=0)
    imp = imp + pltpu.roll(imp, LANES - half, 1)
    n_free = N_SEL - 3
    cand = (lane >= 1) & (lane <= nslc_past - 2)
    score = jnp.where(cand, imp, -jnp.inf)
    rank = jnp.zeros(score.shape, F32)
    for r in range(1, nslc_past):
        rank = rank + jnp.where(pltpu.roll(score, r, 1) >= score, 1.0, 0.0)
        rank = rank + jnp.where(pltpu.roll(score, LANES - r, 1) > score, 1.0, 0.0)
    sel = (cand & (rank < n_free)) | (lane == 0) | (lane == nslc_past - 1)
    sel = jnp.where(sel, 1.0, 0.0)
    selrows = jnp.concatenate([sel[g * tn:(g + 1) * tn] for g in range(N_KV_NSA) for _ in range(4)],
                              axis=0).astype(BF16)

    brow = lax.broadcasted_iota(jnp.int32, (LANES, page), 0)
    kcol = lax.broadcasted_iota(jnp.int32, (LANES, page), 1)
    new_ok = (lane <= tq_row) & (lane < tn)
    s_tiles = []
    for j in range(n_pages):
        s = _nt(qb, nsa_pages[j][:, 256:384].astype(BF16))
        expand = jnp.where(brow == ((j * page + kcol) >> 6), 1.0, 0.0).astype(BF16)
        s_tiles.append(jnp.where(_mm(selrows, expand) > 0.5, s, NEG_INF))
    ksn = _pad_rows(nsan_ref[:, 256:384], LANES).astype(BF16)
    vsn = _pad_rows(nsan_ref[:, 384:512], LANES).astype(BF16)
    s_tiles.append(jnp.where(new_ok, _nt(qb, ksn), NEG_INF))
    mx = functools.reduce(jnp.maximum, [jnp.max(s, axis=-1, keepdims=True) for s in s_tiles])
    p_tiles = [jnp.exp(s - mx) for s in s_tiles]
    den = sum(jnp.sum(p, axis=-1, keepdims=True) for p in p_tiles)
    osel = _mm(p_tiles[n_pages].astype(BF16), vsn)
    for j in range(n_pages):
        osel = osel + _mm(p_tiles[j].astype(BF16), nsa_pages[j][:, 384:512].astype(BF16))
    osel = osel / den

    nwin = win_ref.shape[0]
    wcol = lax.broadcasted_iota(jnp.int32, (1, nwin), 1)
    dist = (nwin + tq_row) - wcol
    mwin = (dist >= 0) & (dist < WINDOW)
    s_w = jnp.where(mwin, _nt(qb, win_ref[:, 0:128].astype(BF16)), NEG_INF)
    kwn = _pad_rows(winn_ref[:, 0:128], LANES).astype(BF16)
    vwn = _pad_rows(winn_ref[:, 128:256], LANES).astype(BF16)
    s_wn = jnp.where(new_ok, _nt(qb, kwn), NEG_INF)
    mxw = jnp.maximum(jnp.max(s_w, axis=-1, keepdims=True), jnp.max(s_wn, axis=-1, keepdims=True))
    p_w = jnp.exp(s_w - mxw)
    p_wn = jnp.exp(s_wn - mxw)
    denw = jnp.sum(p_w, axis=-1, keepdims=True) + jnp.sum(p_wn, axis=-1, keepdims=True)
    ow = (_mm(p_w.astype(BF16), win_ref[:, 128:256].astype(BF16)) + _mm(p_wn.astype(BF16), vwn)) / denw

    gates = gn_ref[...]
    for s in range(4):
        parts = []
        for g in range(N_KV_NSA):
            hd = s + 4 * g
            r = slice(g * gq + s * tn, g * gq + (s + 1) * tn)
            parts.append(gates[:, 3 * hd:3 * hd + 1] * oc[r] + gates[:, 3 * hd + 1:3 * hd + 2] * osel[r]
                         + gates[:, 3 * hd + 2:3 * hd + 3] * ow[r])
        on_ref[:, s * LANES:(s + 1) * LANES] = jnp.where(lane < HEAD_DIM, parts[0], parts[1])

    wout_ref[0:nwin - tn, :] = win_ref[tn:nwin, :]
    wout_ref[nwin - tn:nwin, :] = winn_ref[...]


def _sample_sb_dense_body(pt_ref, sbq_ref, sbn_ref, *rest, n_pages, page):
    sb_pages = rest[:n_pages]
    os_ref = rest[n_pages]
    del pt_ref
    tn = sbq_ref.shape[0]
    nh = N_HEADS_SB
    hrow = lax.broadcasted_iota(jnp.int32, (nh * tn, 1), 0)
    lane_w = lax.broadcasted_iota(jnp.int32, (1, SB_WIDTH), 1)
    sq = sbq_ref[...].astype(F32)
    qbd = jnp.where((lane_w >> 6) == (hrow >> 3), jnp.concatenate([sq] * nh, axis=0), 0.0).astype(BF16)
    tq_h = hrow & (tn - 1)
    upper = _suffix_matrix(page)
    kn = _pad_rows(sbn_ref[:, 0:SB_WIDTH], page).astype(BF16)
    vn = _pad_rows(sbn_ref[:, SB_WIDTH:], page).astype(BF16)
    lane_p = lax.broadcasted_iota(jnp.int32, (1, page), 1)
    causal = lane_p < tq_h
    sp, lsig = _softplus_parts(_nt(qbd, kn))
    sp = jnp.where(causal, sp, 0.0)
    hi, lo = _split_hi_lo(sp)
    later = _mm(hi, upper) + _mm(lo, upper)
    a = jnp.where(causal, jnp.exp(lsig - later), 0.0)
    acc = _mm(a.astype(BF16), vn)
    carry = jnp.sum(sp, axis=-1, keepdims=True)
    for j in range(n_pages - 1, -1, -1):
        pg = sb_pages[j]
        sp, lsig = _softplus_parts(_nt(qbd, pg[:, 0:SB_WIDTH].astype(BF16)))
        hi, lo = _split_hi_lo(sp)
        later = _mm(hi, upper) + _mm(lo, upper) + carry
        acc = acc + _mm(jnp.exp(lsig - later).astype(BF16), pg[:, SB_WIDTH:].astype(BF16))
        carry = carry + jnp.sum(sp, axis=-1, keepdims=True)
    o = jnp.zeros((tn, SB_WIDTH), F32)
    for hd in range(nh):
        o = o + jnp.where((lane_w >> 6) == hd, acc[hd * tn:(hd + 1) * tn], 0.0)
    os_ref[...] = o


def _sample_nsa_call(page_table, q, gn, nsa_new, win_new, wc_t, state_win, cache_nsa, layer):
    nseq, tn, _ = q.shape
    n_pages = page_table.shape[1]
    page = cache_nsa.shape[2]
    nwin = state_win.shape[2]
    assert (n_pages * page) % SLC_BLOCK == 0 and tn == 8 and nwin == WINDOW and SLC_BLOCK == 64
    assert (n_pages * page) // SLC_BLOCK - 2 >= N_SEL - 3
    tok = lambda w: pl.BlockSpec((None, tn, w), lambda b, pt: (b, 0, 0))
    in_specs = ([tok(512), tok(128), tok(512), tok(256),
                 pl.BlockSpec((None, 2, CMP_BLOCK, LANES), lambda b, pt: (layer, 0, 0, 0)),
                 pl.BlockSpec((None, None, nwin, 256), lambda b, pt: (layer, b, 0, 0))]
                + [pl.BlockSpec((None, None, page, 512), lambda b, pt, j=j: (layer, pt[b, j], 0, 0))
                   for j in range(n_pages)])
    out_shape = [jax.ShapeDtypeStruct((nseq, tn, 512), F32), jax.ShapeDtypeStruct((nseq, nwin, 256), F32)]
    out_specs = [tok(512), pl.BlockSpec((None, nwin, 256), lambda b, pt: (b, 0, 0))]
    grid_spec = pltpu.PrefetchScalarGridSpec(
        num_scalar_prefetch=1, grid=(nseq,), in_specs=in_specs, out_specs=out_specs,
        scratch_shapes=[pltpu.VMEM((LANES, LANES), F32), pltpu.VMEM((LANES, LANES), F32)])
    return pl.pallas_call(
        functools.partial(_sample_nsa_body, n_pages=n_pages, page=page),
        out_shape=out_shape,
        grid_spec=grid_spec,
        compiler_params=_cparams(("parallel",)),
        name="sample_nsa",
    )(page_table, q, gn, nsa_new, win_new, wc_t, state_win, *([cache_nsa] * n_pages))


def _sample_sb_dense_call(page_table, sbq, sb_new, cache_sb_l):
    nseq, tn, _ = sbq.shape
    n_pages = page_table.shape[1]
    page = cache_sb_l.shape[1]
    tok = lambda w: pl.BlockSpec((None, tn, w), lambda b, pt: (b, 0, 0))
    in_specs = ([tok(512), tok(1024)]
                + [pl.BlockSpec((None, page, 1024), lambda b, pt, j=j: (pt[b, j], 0, 0))
                   for j in range(n_pages)])
    grid_spec = pltpu.PrefetchScalarGridSpec(
        num_scalar_prefetch=1, grid=(nseq,), in_specs=in_specs, out_specs=tok(512))
    return pl.pallas_call(
        functools.partial(_sample_sb_dense_body, n_pages=n_pages, page=page),
        out_shape=jax.ShapeDtypeStruct((nseq, tn, 512), F32),
        grid_spec=grid_spec,
        compiler_params=_cparams(("parallel",)),
        name="sample_sb_dense",
    )(page_table, sbq, sb_new, *([cache_sb_l] * n_pages))


SB_RECENT_PAGES = 2


def _sample_sb_recent_body(pt_ref, q_ref, kn_ref, vn_ref, *rest, page):
    pages = rest[:SB_RECENT_PAGES]
    o_ref, low_ref = rest[SB_RECENT_PAGES:]
    del pt_ref
    nh = N_HEADS_SB
    nrow = q_ref.shape[0]
    tn = nrow // nh
    lane = lax.broadcasted_iota(jnp.int32, (1, LANES), 1)
    lane_h = lax.broadcasted_iota(jnp.int32, (1, HEAD_DIM), 1)
    row = lax.broadcasted_iota(jnp.int32, (nrow, 1), 0)
    rgrp = row >> 3
    tq = row & (tn - 1)
    qb = q_ref[...].astype(BF16)
    upper = _suffix_matrix(page)

    zn = _nt(qb, kn_ref[...].astype(BF16))
    zn = jnp.concatenate([zn, jnp.zeros((nrow, LANES - nrow), F32)], axis=1)
    own = (lane >> 3) == rgrp
    zn = jnp.where(own, zn, 0.0)
    z0 = jnp.zeros((nrow, LANES), F32)
    for hd in range(nh):
        shifted = zn if hd == 0 else pltpu.roll(zn, LANES - hd * tn, 1)
        z0 = jnp.where(rgrp == hd, shifted, z0)
    causal = lane < tq
    sp, lsig = _softplus_parts(z0)
    sp = jnp.where(causal, sp, 0.0)
    hi, lo = _split_hi_lo(sp)
    a = jnp.where(causal, jnp.exp(lsig - (_mm(hi, upper) + _mm(lo, upper))), 0.0)
    abd = jnp.zeros((nrow, LANES), F32)
    for hd in range(nh):
        shifted = a if hd == 0 else pltpu.roll(a, hd * tn, 1)
        abd = jnp.where(rgrp == hd, shifted, abd)
    acc = _mm(abd[:, 0:nrow].astype(BF16), vn_ref[...].astype(BF16))
    carry = jnp.sum(sp, axis=-1, keepdims=True)

    for pg in pages:
        k_st = jnp.concatenate([pg[pl.ds(hd, page, stride=2 * nh), :] for hd in range(nh)], axis=0)
        v_st = jnp.concatenate([pg[pl.ds(nh + hd, page, stride=2 * nh), :] for hd in range(nh)], axis=0)
        z_all = _nt(qb, k_st.astype(BF16))
        z = jnp.concatenate([z_all[hd * tn:(hd + 1) * tn, hd * page:(hd + 1) * page] for hd in range(nh)],
                            axis=0)
        sp, lsig = _softplus_parts(z)
        hi, lo = _split_hi_lo(sp)
        a = jnp.exp(lsig - (_mm(hi, upper) + _mm(lo, upper) + carry))
        a_bd = jnp.concatenate([jnp.where(rgrp == hd, a, 0.0) for hd in range(nh)], axis=1)
        acc = acc + _mm(a_bd.astype(BF16), v_st.astype(BF16))
        carry = carry + jnp.sum(sp, axis=-1, keepdims=True)
    del lane_h
    o_ref[...] = acc
    low_ref[...] = jnp.full(low_ref.shape, jnp.min(carry), F32)


def _sample_sb_recent_call(page_table, q_hm, kn_hm, vn_hm, cache_sb_rows, layer):
    nseq, nrow, _ = q_hm.shape
    n_pages = page_table.shape[1]
    rows = cache_sb_rows.shape[2]
    page = rows // (2 * N_HEADS_SB)
    assert page == LANES and nrow == 64
    tok = pl.BlockSpec((None, nrow, HEAD_DIM), lambda b, pt: (b, 0, 0))
    in_specs = [tok, tok, tok] + [
        pl.BlockSpec((None, None, rows, HEAD_DIM), lambda b, pt, j=j: (layer, pt[b, n_pages - 1 - j], 0, 0))
        for j in range(SB_RECENT_PAGES)]
    grid_spec = pltpu.PrefetchScalarGridSpec(
        num_scalar_prefetch=1, grid=(nseq,), in_specs=in_specs,
        out_specs=[tok, pl.BlockSpec((None, 8, LANES), lambda b, pt: (b, 0, 0))])
    return pl.pallas_call(
        functools.partial(_sample_sb_recent_body, page=page),
        out_shape=[jax.ShapeDtypeStruct((nseq, nrow, HEAD_DIM), F32),
                   jax.ShapeDtypeStruct((nseq, 8, LANES), F32)],
        grid_spec=grid_spec,
        compiler_params=_cparams(("parallel",)),
        name="sample_sb_recent",
    )(page_table, q_hm, kn_hm, vn_hm, *([cache_sb_rows] * SB_RECENT_PAGES))


def _post_body(x_ref, gt_ref, on_ref, os_ref, gm_ref, wa_ref, wb_ref, wo_ref, o_ref):
    ya = _mm(on_ref[...].astype(BF16), wa_ref[...])
    yb = _mm(os_ref[...].astype(BF16), wb_ref[...])
    m = gm_ref[:, 0:D_MODEL] * ya + gm_ref[:, D_MODEL:] * yb
    o_ref[...] = x_ref[...] + gt_ref[...] * _mm(m.astype(BF16), wo_ref[...])


def _post_call(x, gt, on, osb, gm, wa, wb, wo, layer, *, tm):
    bsz, t, _ = x.shape
    mrows = gt.shape[1]
    mblk = 1 if mrows == 1 else tm
    mod_map = (lambda b, i: (b, 0, 0)) if mrows == 1 else (lambda b, i: (b, i, 0))
    tok = lambda w: pl.BlockSpec((None, tm, w), lambda b, i: (b, i, 0))
    wspec = lambda k: pl.BlockSpec((None, k, D_MODEL), lambda b, i: (layer, 0, 0))
    return pl.pallas_call(
        _post_body,
        out_shape=jax.ShapeDtypeStruct(x.shape, F32),
        grid=(bsz, t // tm),
        in_specs=[tok(D_MODEL), pl.BlockSpec((None, mblk, D_MODEL), mod_map), tok(512), tok(512),
                  tok(2048), wspec(512), wspec(512), wspec(D_MODEL)],
        out_specs=tok(D_MODEL),
        compiler_params=_cparams(("parallel", "parallel")),
        name="post_attn",
    )(x, gt, on, osb, gm, wa, wb, wo)


def _moe_body(x_ref, sc_ref, sh_ref, gt_ref, g_ref, wr_ref, br_ref, w1_ref, w3_ref, w2_ref, o_ref,
              h_s, gate_s, acc_s):
    e = pl.program_id(2)
    lane = lax.broadcasted_iota(jnp.int32, (1, LANES), 1)
    lanef = lane.astype(F32)

    @pl.when(e == 0)
    def _():
        x = x_ref[...]
        ms = jnp.mean(x * x, axis=-1, keepdims=True)
        h = (x * lax.rsqrt(ms + EPS)) * g_ref[...]
        h = h * (1.0 + sc_ref[...]) + sh_ref[...]
        hb = h.astype(BF16)
        h_s[...] = hb
        logit = _mm(hb, wr_ref[...]) + br_ref[...]
        is_g = lane < N_GROUPS
        lg = jnp.where(is_g, logit, -jnp.inf)
        gmax = jnp.max(lg, axis=-1, keepdims=True)
        grp = jnp.min(jnp.where(lg == gmax, lanef, float(LANES)), axis=-1, keepdims=True)
        p_grp = 1.0 / jnp.sum(jnp.exp(lg - gmax), axis=-1, keepdims=True)
        lo_l = N_GROUPS + grp * EXPERTS_PER_GROUP
        in_grp = (lanef >= lo_l) & (lanef < lo_l + EXPERTS_PER_GROUP)
        le = jnp.where(in_grp, logit, -jnp.inf)
        v1 = jnp.max(le, axis=-1, keepdims=True)
        i1 = jnp.min(jnp.where(le == v1, lanef, float(LANES)), axis=-1, keepdims=True)
        le2 = jnp.where(lanef == i1, -jnp.inf, le)
        v2 = jnp.max(le2, axis=-1, keepdims=True)
        i2 = jnp.min(jnp.where(le2 == v2, lanef, float(LANES)), axis=-1, keepdims=True)
        e2 = jnp.exp(v2 - v1)
        w1 = p_grp * (1.0 / (1.0 + e2))
        w2 = p_grp * (e2 / (1.0 + e2))
        gate_s[...] = jnp.where(lanef == i1, w1, jnp.where(lanef == i2, w2, 0.0))
        acc_s[...] = jnp.zeros(acc_s.shape, F32)

    hb = h_s[...]
    ge = jnp.sum(jnp.where(lane == N_GROUPS + e, gate_s[...], 0.0), axis=-1, keepdims=True)
    u = _mm(hb, w1_ref[...])
    hid = (u * jax.nn.sigmoid(u)) * _mm(hb, w3_ref[...]) * ge
    acc_s[...] += _mm(hid.astype(BF16), w2_ref[...])

    @pl.when(e == pl.num_programs(2) - 1)
    def _():
        o_ref[...] = x_ref[...] + gt_ref[...] * acc_s[...]


def _moe_call(x, sc, sh, gt, g_norm, wr, br, w1, w3, w2, layer, *, tm):
    bsz, t, _ = x.shape
    mrows = sc.shape[1]
    mblk = 1 if mrows == 1 else tm
    mod_map = (lambda b, i, e: (b, 0, 0)) if mrows == 1 else (lambda b, i, e: (b, i, 0))
    tok = pl.BlockSpec((None, tm, D_MODEL), lambda b, i, e: (b, i, 0))
    mod = pl.BlockSpec((None, mblk, D_MODEL), mod_map)
    return pl.pallas_call(
        _moe_body,
        out_shape=jax.ShapeDtypeStruct(x.shape, F32),
        grid=(bsz, t // tm, N_EXPERTS),
        in_specs=[tok, mod, mod, mod,
                  pl.BlockSpec((None, None, 1, D_MODEL), lambda b, i, e: (layer, 1, 0, 0)),
                  pl.BlockSpec((None, D_MODEL, LANES), lambda b, i, e: (layer, 0, 0)),
                  pl.BlockSpec((None, 1, LANES), lambda b, i, e: (layer, 0, 0)),
                  pl.BlockSpec((None, None, D_MODEL, D_EXPERT), lambda b, i, e: (layer, e, 0, 0)),
                  pl.BlockSpec((None, None, D_MODEL, D_EXPERT), lambda b, i, e: (layer, e, 0, 0)),
                  pl.BlockSpec((None, None, D_EXPERT, D_MODEL), lambda b, i, e: (layer, e, 0, 0))],
        out_specs=tok,
        scratch_shapes=[pltpu.VMEM((tm, D_MODEL), BF16), pltpu.VMEM((tm, LANES), F32),
                        pltpu.VMEM((tm, D_MODEL), F32)],
        compiler_params=_cparams(("parallel", "parallel", "arbitrary")),
        name="moe",
    )(x, sc, sh, gt, g_norm, wr, br, w1, w3, w2)


def _rope_tables(pos):
    inv_freq = ROPE_THETA ** (-jnp.arange(0, HEAD_DIM, 2, dtype=F32) / HEAD_DIM)
    ang = pos.astype(F32)[:, None] * inv_freq[None, :]
    cos = jnp.cos(ang)
    sin = jnp.sin(ang)
    cos_t = jnp.concatenate([cos, cos, cos, cos], axis=1)
    sin_t = jnp.concatenate([-sin, sin, -sin, sin], axis=1)
    return cos_t, sin_t


def _even_odd(x):
    return jnp.concatenate([x[:, 0::2], x[:, 1::2]], axis=1)


def _tile(n, pref):
    return pref if n % pref == 0 else n


def _forward(x_prompt, x_sample, cache_nsa, cache_sb, state_win, page_table, c_prompt, c_sample,
             w_ada, b_ada, g_norm, w_in, w_qk_norm, w_cmp, w_br_a, w_br_b, w_out,
             w_route_group, b_route_group, w_route_expert, b_route_expert,
             w_expert_gate, w_expert_up, w_expert_down):
    depth = w_in.shape[0]
    bp, seq, _ = x_prompt.shape
    nseq, tn, _ = x_sample.shape
    n_phys, page = cache_nsa.shape[1], cache_nsa.shape[2]
    past_len = page_table.shape[1] * page

    pad_gn = jnp.zeros((depth, D_MODEL, LANES - 3 * N_HEADS_NSA), F32)
    w_in_p = jnp.concatenate(
        [w_in[:, :, _QPERM], w_in[:, :, 512:1280], w_in[:, :, 1280:1304], pad_gn, w_in[:, :, 1304:]],
        axis=-1).astype(BF16)
    qkn_t = jnp.concatenate([w_qk_norm, w_qk_norm], axis=-1)
    wc_t = jnp.concatenate([w_cmp, w_cmp], axis=-1)
    wa_p = w_br_a[:, _QPERM, :].astype(BF16)
    wb_p = w_br_b.astype(BF16)
    wo_p = w_out.astype(BF16)
    pad_r = jnp.zeros((depth, D_MODEL, LANES - N_GROUPS - N_EXPERTS), F32)
    wr_p = jnp.concatenate([w_route_group, w_route_expert, pad_r], axis=-1).astype(BF16)
    br_p = jnp.concatenate([b_route_group, b_route_expert,
                            jnp.zeros((depth, LANES - N_GROUPS - N_EXPERTS), F32)], axis=-1)
    br_p = br_p.reshape(depth, 1, LANES)
    w1_p = w_expert_gate.astype(BF16)
    w3_p = w_expert_up.astype(BF16)
    w2_p = w_expert_down.astype(BF16)
    g_norm4 = g_norm.reshape(depth, 2, 1, D_MODEL)
    r = lax.broadcasted_iota(jnp.int32, (2 * LANES, LANES), 0)
    c = lax.broadcasted_iota(jnp.int32, (2 * LANES, LANES), 1)
    bd = jnp.where((r % LANES) // HEAD_DIM == c // HEAD_DIM, 1.0, 0.0).astype(BF16)

    cos_p, sin_p = _rope_tables(jnp.arange(seq))
    cos_s, sin_s = _rope_tables(past_len + (jnp.arange(nseq * tn) % tn))

    cache_nsa_v = cache_nsa.reshape(depth, n_phys, page, 4 * N_KV_NSA * HEAD_DIM)
    cache_sb_rows = cache_sb.reshape(depth, n_phys, page * 2 * N_HEADS_SB, HEAD_DIM)
    state_win_v = state_win.reshape(depth, nseq, state_win.shape[2], 2 * N_KV_NSA * HEAD_DIM)

    n_c = bp + nseq
    n_cp = -(-n_c // 8) * 8
    c_all = jnp.concatenate([c_prompt, c_sample, jnp.zeros((n_cp - n_c, D_MODEL), F32)], axis=0)
    mod = _ada_call(c_all, w_ada, b_ada)

    def mods(l):
        mp = mod[l, 0:bp].reshape(bp, 1, 6, D_MODEL)
        ms_ = jnp.broadcast_to(mod[l, bp:n_c].reshape(nseq, 1, 6, D_MODEL), (nseq, tn, 6, D_MODEL))
        ms_ = ms_.reshape(1, nseq * tn, 6, D_MODEL)
        return [mp[:, :, k] for k in range(6)], [ms_[:, :, k] for k in range(6)]

    xp = x_prompt
    xs = x_sample.reshape(1, nseq * tn, D_MODEL)
    tm_p = _tile(seq, 256)
    tm_s = _tile(nseq * tn, 256)
    outs = [[] for _ in range(6)]
    for l in range(depth):
        (sh1, sc1, gt1, sh2, sc2, gt2), (sh1s, sc1s, gt1s, sh2s, sc2s, gt2s) = mods(l)

        (q, nsa, nsab, win, winb, gn, sbq, sbkv, sbkvb, gm, kc, vc) = _inproj_call(
            xp, sc1, sh1, g_norm4, w_in_p, qkn_t, cos_p, sin_p, bd, wc_t, l, tm=tm_p, with_cmp=True,
            qdtype=BF16)
        o_n = _nsa_prompt_call(q, gn, _even_odd(kc).astype(BF16), _even_odd(vc).astype(BF16), nsab, winb)
        o_s = _sb_prompt_call(sbq, sbkvb)
        xp = _post_call(xp, gt1, o_n, o_s, gm, wa_p, wb_p, wo_p, l, tm=_tile(seq, 512))
        xp = _moe_call(xp, sc2, sh2, gt2, g_norm4, wr_p, br_p, w1_p, w3_p, w2_p, l, tm=_tile(seq, 1024))
        outs[0].append(nsa.reshape(bp, seq, 4, N_KV_NSA, HEAD_DIM))
        outs[1].append(sbkv.reshape(bp, seq, 2, N_HEADS_SB, HEAD_DIM))
        nw = min(WINDOW, seq)
        outs[2].append(win[:, seq - nw:].reshape(bp, nw, 2, N_KV_NSA, HEAD_DIM))

        (qs_, nsas, _, wins, _, gns, sbqs, sbkvs, _, gms) = _inproj_call(
            xs, sc1s, sh1s, g_norm4, w_in_p, qkn_t, cos_s, sin_s, bd, wc_t, l, tm=tm_s, with_cmp=False,
            qdtype=F32)
        sq = lambda a: a.reshape(nseq, tn, a.shape[-1])
        o_ns, win_new = _sample_nsa_call(
            page_table, sq(qs_), sq(gns), sq(nsas), sq(wins), wc_t, state_win_v, cache_nsa_v, l)

        head_major = lambda a: jnp.swapaxes(a.reshape(nseq, tn, N_HEADS_SB, HEAD_DIM), 1, 2).reshape(
            nseq, N_HEADS_SB * tn, HEAD_DIM)
        o_hm, low = _sample_sb_recent_call(
            page_table, head_major(sbqs), head_major(sbkvs[..., 0:SB_WIDTH]),
            head_major(sbkvs[..., SB_WIDTH:]), cache_sb_rows, l)
        o_short = jnp.swapaxes(o_hm.reshape(nseq, N_HEADS_SB, tn, HEAD_DIM), 1, 2).reshape(nseq, tn, SB_WIDTH)

        def full_sweep(layer=l, sbqs=sbqs, sbkvs=sbkvs):
            dense = cache_sb[layer].reshape(n_phys, page, 2 * N_HEADS_SB * HEAD_DIM)
            return _sample_sb_dense_call(page_table, sq(sbqs), sq(sbkvs), dense)

        o_ss = lax.cond(jnp.min(low) >= SB_UNDERFLOW, lambda: o_short, full_sweep)
        xs = _post_call(xs, gt1s, o_ns.reshape(1, nseq * tn, 512), o_ss.reshape(1, nseq * tn, 512), gms,
                        wa_p, wb_p, wo_p, l, tm=tm_s)
        xs = _moe_call(xs, sc2s, sh2s, gt2s, g_norm4, wr_p, br_p, w1_p, w3_p, w2_p, l, tm=tm_s)
        outs[3].append(nsas.reshape(nseq, tn, 4, N_KV_NSA, HEAD_DIM))
        outs[4].append(sbkvs.reshape(nseq, tn, 2, N_HEADS_SB, HEAD_DIM))
        outs[5].append(win_new.reshape(nseq, win_new.shape[1], 2, N_KV_NSA, HEAD_DIM))

    return (xp, xs.reshape(nseq, tn, D_MODEL)) + tuple(jnp.stack(o) for o in outs)


def kernel(x_prompt, x_sample, cache_nsa, cache_sb, state_win, page_table, c_prompt, c_sample, w_ada, b_ada, g_norm, w_in, w_qk_norm, w_cmp, w_br_a, w_br_b, w_out, w_route_group, b_route_group, w_route_expert, b_route_expert, w_expert_gate, w_expert_up, w_expert_down):
    return _forward(x_prompt, x_sample, cache_nsa, cache_sb, state_win, page_table, c_prompt, c_sample,
                    w_ada, b_ada, g_norm, w_in, w_qk_norm, w_cmp, w_br_a, w_br_b, w_out,
                    w_route_group, b_route_group, w_route_expert, b_route_expert,
                    w_expert_gate, w_expert_up, w_expert_down)
```

```python
import functools

import numpy as np
import jax
import jax.numpy as jnp
from jax import lax
from jax.experimental import pallas as pl
from jax.experimental.pallas import tpu as pltpu

F32 = jnp.float32
BF16 = jnp.bfloat16

D_MODEL = 1024
HEAD_DIM = 64
N_HEADS_NSA = 8
N_KV_NSA = 2
GROUP_SIZE = 4
N_HEADS_SB = 8
NSA_WIDTH = 512
SB_WIDTH = 512
CMP_BLOCK = 32
SLC_BLOCK = 64
N_SEL = 16
WINDOW = 512
ROPE_THETA = 10000.0
N_GROUPS = 4
EXPERTS_PER_GROUP = 4
N_EXPERTS = 16
D_EXPERT = 256
EPS = 1e-6
FORCE_SCORE = 1e4
NEG_INF = -1e30
SCALE = HEAD_DIM ** -0.5

LANES = 128
C_Q = 0
C_KV = 512
C_GN = 1280
C_SB = 1408
C_GM = 2944
C_END = 4992
VMEM_LIMIT = 56 * 1024 * 1024

_QPERM = np.concatenate([np.arange(h * 64, (h + 1) * 64) for s in range(4) for h in (s, 4 + s)])


def _nt(a, b):
    return lax.dot_general(a, b, (((1,), (1,)), ((), ())), preferred_element_type=F32)


def _mm(a, b):
    return jnp.dot(a, b, preferred_element_type=F32)


def _split_hi_lo(x):
    hi = x.astype(BF16)
    lo = (x - hi.astype(F32)).astype(BF16)
    return hi, lo


def _cparams(sem):
    return pltpu.CompilerParams(dimension_semantics=sem, vmem_limit_bytes=VMEM_LIMIT)


def _ada_body(c_ref, w_ref, b_ref, o_ref):
    c = c_ref[...]
    a = (c * jax.nn.sigmoid(c)).astype(BF16)
    o_ref[...] = _mm(a, w_ref[...].astype(BF16)) + b_ref[...]


def _ada_call(c_all, w_ada, b_ada):
    depth = w_ada.shape[0]
    n = c_all.shape[0]
    tn = 1536
    return pl.pallas_call(
        _ada_body,
        out_shape=jax.ShapeDtypeStruct((depth, n, 6 * D_MODEL), F32),
        grid=(depth, 6 * D_MODEL // tn),
        in_specs=[pl.BlockSpec((n, D_MODEL), lambda l, j: (0, 0)),
                  pl.BlockSpec((None, D_MODEL, tn), lambda l, j: (l, 0, j)),
                  pl.BlockSpec((None, 1, tn), lambda l, j: (l, 0, j))],
        out_specs=pl.BlockSpec((None, n, tn), lambda l, j: (l, 0, j)),
        compiler_params=_cparams(("parallel", "parallel")),
        name="ada_mod",
    )(c_all, w_ada, b_ada.reshape(depth, 1, 6 * D_MODEL))


def _inproj_body(x_ref, sc_ref, sh_ref, g_ref, w_ref, qkn_ref, cos_ref, sin_ref, bd_ref, wc_ref,
                 *outs, with_cmp):
    if with_cmp:
        (q_ref, nsa_ref, nsab_ref, win_ref, winb_ref, gn_ref, sbq_ref, sbkv_ref, sbkvb_ref,
         gm_ref, kc_ref, vc_ref) = outs
    else:
        (q_ref, nsa_ref, nsab_ref, win_ref, winb_ref, gn_ref, sbq_ref, sbkv_ref, sbkvb_ref,
         gm_ref) = outs
    x = x_ref[...]
    tm = x.shape[0]
    ms = jnp.mean(x * x, axis=-1, keepdims=True)
    h = (x * lax.rsqrt(ms + EPS)) * g_ref[...]
    h = h * (1.0 + sc_ref[...]) + sh_ref[...]
    hb = h.astype(BF16)
    cos = cos_ref[...]
    sin = sin_ref[...]
    bd = bd_ref[...]
    lane = lax.broadcasted_iota(jnp.int32, (1, LANES), 1)
    first_half = (lane & (HEAD_DIM - 1)) < HEAD_DIM // 2

    def seg(lo, hi):
        return _mm(hb, w_ref[:, lo:hi])

    def norm_rope(xs, gq):
        hi, lo = _split_hi_lo(xs * xs)
        ssum = _mm(jnp.concatenate([hi, lo], axis=1), bd)
        y = xs * lax.rsqrt(ssum * (1.0 / HEAD_DIM) + EPS) * gq
        sw = jnp.where(first_half, pltpu.roll(y, 96, 1), pltpu.roll(y, 32, 1))
        return y * cos + sw * sin

    pq = seg(C_Q, C_KV)
    for s in range(4):
        qs = norm_rope(pq[:, s * LANES:(s + 1) * LANES], qkn_ref[0:1, :]) * SCALE
        q_ref[:, s * LANES:(s + 1) * LANES] = qs.astype(q_ref.dtype)

    pkv = seg(C_KV, C_GN)
    kc = norm_rope(pkv[:, 0:128], qkn_ref[1:2, :])
    vc = pkv[:, 128:256]
    ks = norm_rope(pkv[:, 256:384], qkn_ref[2:3, :])
    vs = pkv[:, 384:512]
    kw = norm_rope(pkv[:, 512:640], qkn_ref[3:4, :])
    vw = pkv[:, 640:768]
    for j, v in enumerate((kc, vc, ks, vs)):
        nsa_ref[:, j * LANES:(j + 1) * LANES] = v
        nsab_ref[:, j * LANES:(j + 1) * LANES] = v.astype(BF16)
    for j, v in enumerate((kw, vw)):
        win_ref[:, j * LANES:(j + 1) * LANES] = v
        winb_ref[:, j * LANES:(j + 1) * LANES] = v.astype(BF16)
    if with_cmp:
        nb = tm // CMP_BLOCK
        kc_ref[...] = jnp.sum(kc.reshape(nb, CMP_BLOCK, LANES) * wc_ref[0][None], axis=1)
        vc_ref[...] = jnp.sum(vc.reshape(nb, CMP_BLOCK, LANES) * wc_ref[1][None], axis=1)

    gn_ref[...] = jax.nn.sigmoid(seg(C_GN, C_SB))

    psb = seg(C_SB, C_GM)
    sbq_ref[...] = (psb[:, 0:SB_WIDTH] * SCALE).astype(sbq_ref.dtype)
    sbkv_ref[...] = psb[:, SB_WIDTH:]
    sbkvb_ref[...] = psb[:, SB_WIDTH:].astype(BF16)

    gm_ref[...] = jax.nn.sigmoid(seg(C_GM, C_END))


def _inproj_call(x, sc, sh, g_norm, w_in_p, qkn_t, cos_t, sin_t, bd, wc_t, layer, *, tm, with_cmp,
                 qdtype):
    bsz, t, _ = x.shape
    mrows = sc.shape[1]
    mblk = 1 if mrows == 1 else tm
    mod_map = (lambda b, i: (b, 0, 0)) if mrows == 1 else (lambda b, i: (b, i, 0))
    tok = lambda w: pl.BlockSpec((None, tm, w), lambda b, i: (b, i, 0))
    in_specs = [
        tok(D_MODEL),
        pl.BlockSpec((None, mblk, D_MODEL), mod_map),
        pl.BlockSpec((None, mblk, D_MODEL), mod_map),
        pl.BlockSpec((None, None, 1, D_MODEL), lambda b, i: (layer, 0, 0, 0)),
        pl.BlockSpec((None, D_MODEL, C_END), lambda b, i: (layer, 0, 0)),
        pl.BlockSpec((None, 4, LANES), lambda b, i: (layer, 0, 0)),
        pl.BlockSpec((tm, LANES), lambda b, i: (i, 0)),
        pl.BlockSpec((tm, LANES), lambda b, i: (i, 0)),
        pl.BlockSpec((2 * LANES, LANES), lambda b, i: (0, 0)),
        pl.BlockSpec((None, 2, CMP_BLOCK, LANES), lambda b, i: (layer, 0, 0, 0)),
    ]
    sds = lambda w, dt: jax.ShapeDtypeStruct((bsz, t, w), dt)
    out_shape = [sds(512, qdtype), sds(512, F32), sds(512, BF16), sds(256, F32), sds(256, BF16),
                 sds(128, F32), sds(512, qdtype), sds(1024, F32), sds(1024, BF16), sds(2048, F32)]
    out_specs = [tok(512), tok(512), tok(512), tok(256), tok(256), tok(128), tok(512), tok(1024),
                 tok(1024), tok(2048)]
    if with_cmp:
        nb = tm // CMP_BLOCK
        out_shape += [jax.ShapeDtypeStruct((bsz, t // CMP_BLOCK, LANES), F32)] * 2
        out_specs += [pl.BlockSpec((None, nb, LANES), lambda b, i: (b, i, 0))] * 2
    return pl.pallas_call(
        functools.partial(_inproj_body, with_cmp=with_cmp),
        out_shape=out_shape,
        grid=(bsz, t // tm),
        in_specs=in_specs,
        out_specs=out_specs,
        compiler_params=_cparams(("parallel", "parallel")),
        name="inproj",
    )(x, sc, sh, g_norm, w_in_p, qkn_t, cos_t, sin_t, bd, wc_t)


TQ = 128
TKS = 512


def _masked_softmax(s, mask):
    sm = jnp.where(mask, s, NEG_INF)
    mx = jnp.max(sm, axis=-1, keepdims=True)
    e = jnp.exp(sm - mx)
    den = jnp.sum(e, axis=-1, keepdims=True)
    return jnp.where(mask, e / den, 0.0)


def _top_sel_rows(score, n_pick):
    nbk = score.shape[0]
    jrow = lax.broadcasted_iota(jnp.int32, score.shape, 0).astype(F32)
    sel = jnp.zeros(score.shape, F32)
    for _ in range(n_pick):
        m = jnp.max(score, axis=0, keepdims=True)
        idx = jnp.min(jnp.where(score == m, jrow, float(nbk)), axis=0, keepdims=True)
        pick = jrow == idx
        sel = jnp.where(pick, 1.0, sel)
        score = jnp.where(pick, -jnp.inf, score)
    return sel


def _nsa_prompt_body(q_ref, gn_ref, kc_ref, vc_ref, ks_ref, vs_ref, kw_ref, vw_ref, o_ref,
                     s_buf, m_ref, l_ref, acc_ref, oc_ref, sel_ref):
    i = pl.program_id(1)
    q0 = i * TQ
    ncmp = kc_ref.shape[0]
    nslc = ncmp // 2
    lane = lax.broadcasted_iota(jnp.int32, (1, LANES), 1)
    gates = gn_ref[...]
    posq = q0 + lax.broadcasted_iota(jnp.int32, (TQ, 1), 0)
    posq4 = q0 + (lax.broadcasted_iota(jnp.int32, (GROUP_SIZE * TQ, 1), 0) & (TQ - 1))

    lc = lax.broadcasted_iota(jnp.int32, (1, ncmp), 1)
    blk = jnp.where(lc < nslc, 2 * lc, 2 * (lc - nslc) + 1)
    mc4 = ((blk + 1) * CMP_BLOCK - 1) <= posq4

    jrow = lax.broadcasted_iota(jnp.int32, (nslc, TQ), 0)
    cur = (q0 + lax.broadcasted_iota(jnp.int32, (nslc, TQ), 1)) >> 6
    forced = (jrow == 0) | (jrow == cur) | (jrow == cur - 1)
    valid = jrow <= cur

    wlen = WINDOW + TQ
    wstart = pl.multiple_of(jnp.maximum(q0 - WINDOW, 0), TQ)
    posw = wstart + lax.broadcasted_iota(jnp.int32, (1, wlen), 1)
    dist = posq4 - posw
    mw4 = (dist >= 0) & (dist < WINDOW)

    n_tiles = (q0 + TQ + TKS - 1) // TKS
    lmasks = [lane < HEAD_DIM, lane >= HEAD_DIM]
    zero = jnp.zeros((), q_ref.dtype)

    def q_group(g):
        return jnp.concatenate(
            [jnp.where(lmasks[g], q_ref[:, s * LANES:(s + 1) * LANES], zero) for s in range(4)], axis=0)

    scores = []
    for g in range(N_KV_NSA):
        pc = _masked_softmax(_nt(q_group(g), kc_ref[...]), mc4)
        oc_ref[g] = _mm(pc.astype(BF16), vc_ref[...])
        imp = pc[0:TQ] + pc[TQ:2 * TQ] + pc[2 * TQ:3 * TQ] + pc[3 * TQ:4 * TQ]
        imp = imp[:, 0:nslc] + imp[:, nslc:ncmp]
        if nslc < LANES:
            imp = jnp.concatenate([imp, jnp.zeros((TQ, LANES - nslc), F32)], axis=1)
        imp_t = imp.T[0:nslc, :]
        scores.append(jnp.where(forced, FORCE_SCORE, jnp.where(valid, imp_t, -1.0)))
    sel_t = _top_sel_rows(jnp.concatenate(scores, axis=1), min(N_SEL, nslc))
    for g in range(N_KV_NSA):
        st = sel_t[:, g * TQ:(g + 1) * TQ]
        if nslc < LANES:
            st = jnp.concatenate([st, jnp.zeros((LANES - nslc, TQ), F32)], axis=0)
        sel_ref[g] = st.T.astype(BF16)

    s_buf[...] = jnp.full(s_buf.shape, NEG_INF, F32)
    m_ref[...] = jnp.full(m_ref.shape, NEG_INF, F32)
    l_ref[...] = jnp.zeros(l_ref.shape, F32)
    acc_ref[...] = jnp.zeros(acc_ref.shape, F32)

    def sel_trip(it, c):
        kb0 = pl.multiple_of(jnp.maximum(it - 1, 0) * TKS, TKS)
        ka0 = pl.multiple_of(jnp.minimum(it, n_tiles - 1) * TKS, TKS)
        kpos = ka0 + lax.broadcasted_iota(jnp.int32, (LANES, TKS), 1)
        brow = lax.broadcasted_iota(jnp.int32, (LANES, TKS), 0)
        expand = jnp.where(brow == (kpos >> 6), 1.0, 0.0).astype(BF16)
        causal = kpos[0:1, :] <= posq
        for g in range(N_KV_NSA):
            sa = _nt(q_group(g), ks_ref[pl.ds(ka0, TKS), :])
            picked = _mm(sel_ref[g], expand)
            bias = jnp.where((picked > 0.5) & causal, 0.0, NEG_INF)

            s = s_buf[g]
            m_old = m_ref[g]
            m_new = jnp.maximum(m_old, jnp.max(s, axis=-1, keepdims=True))
            alpha = jnp.exp(m_old - m_new)
            p = jnp.exp(s - m_new)
            l_ref[g] = alpha * l_ref[g] + jnp.sum(p, axis=-1, keepdims=True)
            m_ref[g] = m_new
            acc_ref[g] = alpha * acc_ref[g] + _mm(p.astype(BF16), vs_ref[pl.ds(kb0, TKS), :])
            s_buf[g] = jnp.concatenate(
                [sa[r * TQ:(r + 1) * TQ] + bias for r in range(GROUP_SIZE)], axis=0)
        return c

    lax.fori_loop(0, n_tiles + 1, sel_trip, 0)

    out = [jnp.zeros((TQ, LANES), F32) for _ in range(4)]
    for g in range(N_KV_NSA):
        osel = acc_ref[g] / l_ref[g]
        oc = oc_ref[g]
        pw = _masked_softmax(_nt(q_group(g), kw_ref[pl.ds(wstart, wlen), :]), mw4)
        ow = _mm(pw.astype(BF16), vw_ref[pl.ds(wstart, wlen), :])
        for s in range(4):
            hd = s + 4 * g
            r = slice(s * TQ, (s + 1) * TQ)
            comb = (gates[:, 3 * hd:3 * hd + 1] * oc[r] + gates[:, 3 * hd + 1:3 * hd + 2] * osel[r]
                    + gates[:, 3 * hd + 2:3 * hd + 3] * ow[r])
            out[s] = jnp.where(lmasks[g], comb, out[s])
    for s in range(4):
        o_ref[:, s * LANES:(s + 1) * LANES] = out[s].astype(o_ref.dtype)


def _nsa_prompt_call(q, gn, kcp, vcp, nsab, winb):
    bsz, t, _ = q.shape
    ncmp = kcp.shape[1]
    rows = GROUP_SIZE * TQ
    full = lambda col: pl.BlockSpec((None, t, LANES), lambda b, i, c=col: (b, 0, c))
    return pl.pallas_call(
        _nsa_prompt_body,
        out_shape=jax.ShapeDtypeStruct((bsz, t, NSA_WIDTH), BF16),
        grid=(bsz, t // TQ),
        in_specs=[pl.BlockSpec((None, TQ, NSA_WIDTH), lambda b, i: (b, i, 0)),
                  pl.BlockSpec((None, TQ, LANES), lambda b, i: (b, i, 0)),
                  pl.BlockSpec((None, ncmp, LANES), lambda b, i: (b, 0, 0)),
                  pl.BlockSpec((None, ncmp, LANES), lambda b, i: (b, 0, 0)),
                  full(2), full(3), full(0), full(1)],
        out_specs=pl.BlockSpec((None, TQ, NSA_WIDTH), lambda b, i: (b, i, 0)),
        scratch_shapes=[pltpu.VMEM((N_KV_NSA, rows, TKS), F32), pltpu.VMEM((N_KV_NSA, rows, 1), F32),
                        pltpu.VMEM((N_KV_NSA, rows, 1), F32), pltpu.VMEM((N_KV_NSA, rows, LANES), F32),
                        pltpu.VMEM((N_KV_NSA, rows, LANES), F32), pltpu.VMEM((N_KV_NSA, TQ, LANES), BF16)],
        compiler_params=_cparams(("parallel", "parallel")),
        name="nsa_prompt",
    )(q, gn, kcp, vcp, nsab, nsab, winb, winb)


TKB = 256


def _softplus_parts(z):
    sp = jnp.maximum(z, 0.0) + jnp.log(1.0 + jnp.exp(-jnp.abs(z)))
    return sp, z - sp


def _suffix_matrix(n):
    r = lax.broadcasted_iota(jnp.int32, (n, n), 0)
    c = lax.broadcasted_iota(jnp.int32, (n, n), 1)
    return jnp.where(r > c, 1.0, 0.0).astype(BF16)


SB_SLABS = 2
SB_UNDERFLOW = 100.0


def _sb_prompt_body(q_ref, k_ref, v_ref, o_ref, z_buf, e_buf, g_buf, sp_buf, acc_ref):
    i = pl.program_id(2)
    q0 = i * TQ
    lane = lax.broadcasted_iota(jnp.int32, (1, LANES), 1)
    zero = jnp.zeros((), q_ref.dtype)
    posq2 = q0 + (lax.broadcasted_iota(jnp.int32, (2 * TQ, 1), 0) & (TQ - 1))
    upper = _suffix_matrix(TKB)
    t_last = (q0 + TQ - 1) // TKB

    def q_stack(c):
        q2 = q_ref[:, c * LANES:(c + 1) * LANES]
        return jnp.concatenate([jnp.where(lane < HEAD_DIM, q2, zero),
                                jnp.where(lane >= HEAD_DIM, q2, zero)], axis=0)

    def k_tile(c, t):
        return k_ref[pl.ds(pl.multiple_of(t * TKB, TKB), TKB), c * LANES:(c + 1) * LANES]

    def v_tile(c, t):
        return v_ref[pl.ds(pl.multiple_of(t * TKB, TKB), TKB), c * LANES:(c + 1) * LANES]

    causal = (t_last * TKB + lax.broadcasted_iota(jnp.int32, (1, TKB), 1)) < posq2
    run0 = []
    for c in range(SB_SLABS):
        sp, lsig = _softplus_parts(_nt(q_stack(c), k_tile(c, t_last)))
        sp = jnp.where(causal, sp, 0.0)
        a = jnp.where(causal, jnp.exp(lsig - _mm(sp.astype(BF16), upper)), 0.0)
        acc_ref[c] = _mm(a.astype(BF16), v_tile(c, t_last))
        run0.append(jnp.sum(sp, axis=-1, keepdims=True))

    n_rest = t_last
    z_buf[...] = jnp.zeros(z_buf.shape, F32)
    e_buf[...] = jnp.zeros(e_buf.shape, F32)
    g_buf[...] = jnp.zeros(g_buf.shape, F32)
    sp_buf[...] = jnp.zeros(sp_buf.shape, BF16)

    def body(carry):
        it, limit, runs = carry
        ta = jnp.clip(t_last - 1 - it, 0, t_last)
        jc = it - 3
        wc = jnp.where((jc >= 0) & (jc < n_rest), 1.0, 0.0)
        tc = jnp.clip(t_last - 1 - jc, 0, t_last)
        jb = it - 1
        wb = jnp.where((jb >= 0) & (jb < n_rest), 1.0, 0.0)
        new_runs = []
        for c in range(SB_SLABS):
            z_new = _nt(q_stack(c), k_tile(c, ta))
            acc_ref[c] += wc * _mm(jnp.exp(e_buf[c]).astype(BF16), v_tile(c, tc))
            e_buf[c] = g_buf[c] - _mm(sp_buf[c], upper)
            sp, lsig = _softplus_parts(z_buf[c])
            sp_buf[c] = sp.astype(BF16)
            g_buf[c] = lsig - runs[c]
            z_buf[c] = z_new
            new_runs.append(runs[c] + wb * jnp.sum(sp, axis=-1, keepdims=True))
        low = functools.reduce(jnp.minimum, [jnp.min(r) for r in new_runs])
        limit = jnp.where(low >= SB_UNDERFLOW, jnp.minimum(limit, it + 3), limit)
        return it + 1, limit, tuple(new_runs)

    lax.while_loop(lambda carry: carry[0] < carry[1], body,
                   (jnp.int32(0), jnp.where(n_rest > 0, n_rest + 3, 0), tuple(run0)))
    for c in range(SB_SLABS):
        acc = acc_ref[c]
        o_ref[:, c * LANES:(c + 1) * LANES] = jnp.where(
            lane < HEAD_DIM, acc[0:TQ], acc[TQ:2 * TQ]).astype(o_ref.dtype)


def _sb_prompt_call(sbq, sbkvb):
    bsz, t, _ = sbq.shape
    w = SB_SLABS * LANES
    ngrp = SB_WIDTH // w
    buf = lambda last, dt: pltpu.VMEM((SB_SLABS, 2 * TQ, last), dt)
    return pl.pallas_call(
        _sb_prompt_body,
        out_shape=jax.ShapeDtypeStruct((bsz, t, SB_WIDTH), BF16),
        grid=(bsz, ngrp, t // TQ),
        in_specs=[pl.BlockSpec((None, TQ, w), lambda b, s, i: (b, i, s)),
                  pl.BlockSpec((None, t, w), lambda b, s, i: (b, 0, s)),
                  pl.BlockSpec((None, t, w), lambda b, s, i: (b, 0, ngrp + s))],
        out_specs=pl.BlockSpec((None, TQ, w), lambda b, s, i: (b, i, s)),
        scratch_shapes=[buf(TKB, F32), buf(TKB, F32), buf(TKB, F32), buf(TKB, BF16), buf(LANES, F32)],
        compiler_params=_cparams(("parallel", "parallel", "parallel")),
        name="sb_prompt",
    )(sbq, sbkvb, sbkvb)


def _pad_rows(x, n):
    return jnp.concatenate([x, jnp.zeros((n - x.shape[0], x.shape[1]), x.dtype)], axis=0)


def _sample_nsa_body(pt_ref, q_ref, gn_ref, nsan_ref, winn_ref, wc_ref, win_ref, *rest, n_pages, page):
    nsa_pages = rest[:n_pages]
    on_ref, wout_ref, kc_s, vc_s = rest[n_pages:]
    del pt_ref
    tn = q_ref.shape[0]
    nrow = N_KV_NSA * GROUP_SIZE * tn
    n_past = n_pages * page
    per_page = page // CMP_BLOCK
    ncmp = n_pages * per_page
    nslc_past = n_past // SLC_BLOCK
    lane = lax.broadcasted_iota(jnp.int32, (1, LANES), 1)
    row = lax.broadcasted_iota(jnp.int32, (nrow, 1), 0)
    tq_row = row & (tn - 1)

    qs = q_ref[...].astype(F32)
    qb = jnp.concatenate(
        [jnp.where((lane < HEAD_DIM) if g == 0 else (lane >= HEAD_DIM), qs[:, s * LANES:(s + 1) * LANES], 0.0)
         for g in range(N_KV_NSA) for s in range(4)], axis=0).astype(BF16)

    kc_s[...] = jnp.zeros(kc_s.shape, F32)
    vc_s[...] = jnp.zeros(vc_s.shape, F32)
    half = ncmp // 2
    for j in range(n_pages):
        pg = nsa_pages[j]
        kcj = jnp.sum(pg[:, 0:128].reshape(per_page, CMP_BLOCK, LANES) * wc_ref[0][None], axis=1)
        vcj = jnp.sum(pg[:, 128:256].reshape(per_page, CMP_BLOCK, LANES) * wc_ref[1][None], axis=1)
        for r in range(per_page):
            n = j * per_page + r
            dst = n // 2 + (half if n % 2 else 0)
            kc_s[dst:dst + 1, :] = kcj[r:r + 1, :]
            vc_s[dst:dst + 1, :] = vcj[r:r + 1, :]
    mc = lane < ncmp
    pc = _masked_softmax(_nt(qb, kc_s[...].astype(BF16)), mc)
    oc = _mm(pc.astype(BF16), vc_s[...].astype(BF16))
    gq = GROUP_SIZE * tn
    imp = jnp.concatenate(
        [sum(pc[g * gq + s * tn:g * gq + (s + 1) * tn] for s in range(4)) for g in range(N_KV_NSA)],
        axis=0)
    imp = imp + pltpu.roll(imp, LANES - half, 1)
    n_free = N_SEL - 3
    cand = (lane >= 1) & (lane <= nslc_past - 2)
    score = jnp.where(cand, imp, -jnp.inf)
    rank = jnp.zeros(score.shape, F32)
    for r in range(1, nslc_past):
        rank = rank + jnp.where(pltpu.roll(score, r, 1) >= score, 1.0, 0.0)
        rank = rank + jnp.where(pltpu.roll(score, LANES - r, 1) > score, 1.0, 0.0)
    sel = (cand & (rank < n_free)) | (lane == 0) | (lane == nslc_past - 1)
    sel = jnp.where(sel, 1.0, 0.0)
    selrows = jnp.concatenate([sel[g * tn:(g + 1) * tn] for g in range(N_KV_NSA) for _ in range(4)],
                              axis=0).astype(BF16)

    brow = lax.broadcasted_iota(jnp.int32, (LANES, page), 0)
    kcol = lax.broadcasted_iota(jnp.int32, (LANES, page), 1)
    new_ok = (lane <= tq_row) & (lane < tn)
    s_tiles = []
    for j in range(n_pages):
        s = _nt(qb, nsa_pages[j][:, 256:384].astype(BF16))
        expand = jnp.where(brow == ((j * page + kcol) >> 6), 1.0, 0.0).astype(BF16)
        s_tiles.append(jnp.where(_mm(selrows, expand) > 0.5, s, NEG_INF))
    ksn = _pad_rows(nsan_ref[:, 256:384], LANES).astype(BF16)
    vsn = _pad_rows(nsan_ref[:, 384:512], LANES).astype(BF16)
    s_tiles.append(jnp.where(new_ok, _nt(qb, ksn), NEG_INF))
    mx = functools.reduce(jnp.maximum, [jnp.max(s, axis=-1, keepdims=True) for s in s_tiles])
    p_tiles = [jnp.exp(s - mx) for s in s_tiles]
    den = sum(jnp.sum(p, axis=-1, keepdims=True) for p in p_tiles)
    osel = _mm(p_tiles[n_pages].astype(BF16), vsn)
    for j in range(n_pages):
        osel = osel + _mm(p_tiles[j].astype(BF16), nsa_pages[j][:, 384:512].astype(BF16))
    osel = osel / den

    nwin = win_ref.shape[0]
    wcol = lax.broadcasted_iota(jnp.int32, (1, nwin), 1)
    dist = (nwin + tq_row) - wcol
    mwin = (dist >= 0) & (dist < WINDOW)
    s_w = jnp.where(mwin, _nt(qb, win_ref[:, 0:128].astype(BF16)), NEG_INF)
    kwn = _pad_rows(winn_ref[:, 0:128], LANES).astype(BF16)
    vwn = _pad_rows(winn_ref[:, 128:256], LANES).astype(BF16)
    s_wn = jnp.where(new_ok, _nt(qb, kwn), NEG_INF)
    mxw = jnp.maximum(jnp.max(s_w, axis=-1, keepdims=True), jnp.max(s_wn, axis=-1, keepdims=True))
    p_w = jnp.exp(s_w - mxw)
    p_wn = jnp.exp(s_wn - mxw)
    denw = jnp.sum(p_w, axis=-1, keepdims=True) + jnp.sum(p_wn, axis=-1, keepdims=True)
    ow = (_mm(p_w.astype(BF16), win_ref[:, 128:256].astype(BF16)) + _mm(p_wn.astype(BF16), vwn)) / denw

    gates = gn_ref[...]
    for s in range(4):
        parts = []
        for g in range(N_KV_NSA):
            hd = s + 4 * g
            r = slice(g * gq + s * tn, g * gq + (s + 1) * tn)
            parts.append(gates[:, 3 * hd:3 * hd + 1] * oc[r] + gates[:, 3 * hd + 1:3 * hd + 2] * osel[r]
                         + gates[:, 3 * hd + 2:3 * hd + 3] * ow[r])
        on_ref[:, s * LANES:(s + 1) * LANES] = jnp.where(lane < HEAD_DIM, parts[0], parts[1])

    wout_ref[0:nwin - tn, :] = win_ref[tn:nwin, :]
    wout_ref[nwin - tn:nwin, :] = winn_ref[...]


def _sample_sb_dense_body(pt_ref, sbq_ref, sbn_ref, *rest, n_pages, page):
    sb_pages = rest[:n_pages]
    os_ref, low_ref = rest[n_pages:]
    del pt_ref
    tn = sbq_ref.shape[0]
    nh = N_HEADS_SB
    hrow = lax.broadcasted_iota(jnp.int32, (nh * tn, 1), 0)
    lane_w = lax.broadcasted_iota(jnp.int32, (1, SB_WIDTH), 1)
    sq = sbq_ref[...].astype(F32)
    qbd = jnp.where((lane_w >> 6) == (hrow >> 3), jnp.concatenate([sq] * nh, axis=0), 0.0).astype(BF16)
    tq_h = hrow & (tn - 1)
    upper = _suffix_matrix(page)
    kn = _pad_rows(sbn_ref[:, 0:SB_WIDTH], page).astype(BF16)
    vn = _pad_rows(sbn_ref[:, SB_WIDTH:], page).astype(BF16)
    lane_p = lax.broadcasted_iota(jnp.int32, (1, page), 1)
    causal = lane_p < tq_h
    sp, lsig = _softplus_parts(_nt(qbd, kn))
    sp = jnp.where(causal, sp, 0.0)
    hi, lo = _split_hi_lo(sp)
    later = _mm(hi, upper) + _mm(lo, upper)
    a = jnp.where(causal, jnp.exp(lsig - later), 0.0)
    acc = _mm(a.astype(BF16), vn)
    carry = jnp.sum(sp, axis=-1, keepdims=True)
    for j in range(n_pages - 1, -1, -1):
        pg = sb_pages[j]
        sp, lsig = _softplus_parts(_nt(qbd, pg[:, 0:SB_WIDTH].astype(BF16)))
        hi, lo = _split_hi_lo(sp)
        later = _mm(hi, upper) + _mm(lo, upper) + carry
        acc = acc + _mm(jnp.exp(lsig - later).astype(BF16), pg[:, SB_WIDTH:].astype(BF16))
        carry = carry + jnp.sum(sp, axis=-1, keepdims=True)
    o = jnp.zeros((tn, SB_WIDTH), F32)
    for hd in range(nh):
        o = o + jnp.where((lane_w >> 6) == hd, acc[hd * tn:(hd + 1) * tn], 0.0)
    os_ref[...] = o
    low_ref[...] = jnp.full(low_ref.shape, jnp.min(carry), F32)


def _sample_nsa_call(page_table, q, gn, nsa_new, win_new, wc_t, state_win, cache_nsa, layer):
    nseq, tn, _ = q.shape
    n_pages = page_table.shape[1]
    page = cache_nsa.shape[2]
    nwin = state_win.shape[2]
    assert (n_pages * page) % SLC_BLOCK == 0 and tn == 8 and nwin == WINDOW and SLC_BLOCK == 64
    assert (n_pages * page) // SLC_BLOCK - 2 >= N_SEL - 3
    tok = lambda w: pl.BlockSpec((None, tn, w), lambda b, pt: (b, 0, 0))
    in_specs = ([tok(512), tok(128), tok(512), tok(256),
                 pl.BlockSpec((None, 2, CMP_BLOCK, LANES), lambda b, pt: (layer, 0, 0, 0)),
                 pl.BlockSpec((None, None, nwin, 256), lambda b, pt: (layer, b, 0, 0))]
                + [pl.BlockSpec((None, None, page, 512), lambda b, pt, j=j: (layer, pt[b, j], 0, 0))
                   for j in range(n_pages)])
    out_shape = [jax.ShapeDtypeStruct((nseq, tn, 512), F32), jax.ShapeDtypeStruct((nseq, nwin, 256), F32)]
    out_specs = [tok(512), pl.BlockSpec((None, nwin, 256), lambda b, pt: (b, 0, 0))]
    grid_spec = pltpu.PrefetchScalarGridSpec(
        num_scalar_prefetch=1, grid=(nseq,), in_specs=in_specs, out_specs=out_specs,
        scratch_shapes=[pltpu.VMEM((LANES, LANES), F32), pltpu.VMEM((LANES, LANES), F32)])
    return pl.pallas_call(
        functools.partial(_sample_nsa_body, n_pages=n_pages, page=page),
        out_shape=out_shape,
        grid_spec=grid_spec,
        compiler_params=_cparams(("parallel",)),
        name="sample_nsa",
    )(page_table, q, gn, nsa_new, win_new, wc_t, state_win, *([cache_nsa] * n_pages))


SB_RECENT_PAGES = 2


def _sample_sb_call(page_table, sbq, sb_new, cache_sb, layer, n_recent):
    nseq, tn, _ = sbq.shape
    n_pages = page_table.shape[1]
    page = cache_sb.shape[2]
    tok = lambda w: pl.BlockSpec((None, tn, w), lambda b, pt: (b, 0, 0))
    in_specs = ([tok(512), tok(1024)]
                + [pl.BlockSpec((None, None, page, 1024), lambda b, pt, j=j: (layer, pt[b, j], 0, 0))
                   for j in range(n_pages - n_recent, n_pages)])
    grid_spec = pltpu.PrefetchScalarGridSpec(
        num_scalar_prefetch=1, grid=(nseq,), in_specs=in_specs,
        out_specs=[tok(512), pl.BlockSpec((None, 8, LANES), lambda b, pt: (b, 0, 0))])
    return pl.pallas_call(
        functools.partial(_sample_sb_dense_body, n_pages=n_recent, page=page),
        out_shape=[jax.ShapeDtypeStruct((nseq, tn, 512), F32), jax.ShapeDtypeStruct((nseq, 8, LANES), F32)],
        grid_spec=grid_spec,
        compiler_params=_cparams(("parallel",)),
        name="sample_sb",
    )(page_table, sbq, sb_new, *([cache_sb] * n_recent))


def _post_body(x_ref, gt_ref, on_ref, os_ref, gm_ref, wa_ref, wb_ref, wo_ref, o_ref):
    ya = _mm(on_ref[...].astype(BF16), wa_ref[...])
    yb = _mm(os_ref[...].astype(BF16), wb_ref[...])
    m = gm_ref[:, 0:D_MODEL] * ya + gm_ref[:, D_MODEL:] * yb
    o_ref[...] = x_ref[...] + gt_ref[...] * _mm(m.astype(BF16), wo_ref[...])


def _post_call(x, gt, on, osb, gm, wa, wb, wo, layer, *, tm):
    bsz, t, _ = x.shape
    mrows = gt.shape[1]
    mblk = 1 if mrows == 1 else tm
    mod_map = (lambda b, i: (b, 0, 0)) if mrows == 1 else (lambda b, i: (b, i, 0))
    tok = lambda w: pl.BlockSpec((None, tm, w), lambda b, i: (b, i, 0))
    wspec = lambda k: pl.BlockSpec((None, k, D_MODEL), lambda b, i: (layer, 0, 0))
    return pl.pallas_call(
        _post_body,
        out_shape=jax.ShapeDtypeStruct(x.shape, F32),
        grid=(bsz, t // tm),
        in_specs=[tok(D_MODEL), pl.BlockSpec((None, mblk, D_MODEL), mod_map), tok(512), tok(512),
                  tok(2048), wspec(512), wspec(512), wspec(D_MODEL)],
        out_specs=tok(D_MODEL),
        compiler_params=_cparams(("parallel", "parallel")),
        name="post_attn",
    )(x, gt, on, osb, gm, wa, wb, wo)


def _moe_body(x_ref, sc_ref, sh_ref, gt_ref, g_ref, wr_ref, br_ref, w1_ref, w3_ref, w2_ref, o_ref,
              h_s, gate_s, acc_s):
    e = pl.program_id(2)
    lane = lax.broadcasted_iota(jnp.int32, (1, LANES), 1)
    lanef = lane.astype(F32)

    @pl.when(e == 0)
    def _():
        x = x_ref[...]
        ms = jnp.mean(x * x, axis=-1, keepdims=True)
        h = (x * lax.rsqrt(ms + EPS)) * g_ref[...]
        h = h * (1.0 + sc_ref[...]) + sh_ref[...]
        hb = h.astype(BF16)
        h_s[...] = hb
        logit = _mm(hb, wr_ref[...]) + br_ref[...]
        is_g = lane < N_GROUPS
        lg = jnp.where(is_g, logit, -jnp.inf)
        gmax = jnp.max(lg, axis=-1, keepdims=True)
        grp = jnp.min(jnp.where(lg == gmax, lanef, float(LANES)), axis=-1, keepdims=True)
        p_grp = 1.0 / jnp.sum(jnp.exp(lg - gmax), axis=-1, keepdims=True)
        lo_l = N_GROUPS + grp * EXPERTS_PER_GROUP
        in_grp = (lanef >= lo_l) & (lanef < lo_l + EXPERTS_PER_GROUP)
        le = jnp.where(in_grp, logit, -jnp.inf)
        v1 = jnp.max(le, axis=-1, keepdims=True)
        i1 = jnp.min(jnp.where(le == v1, lanef, float(LANES)), axis=-1, keepdims=True)
        le2 = jnp.where(lanef == i1, -jnp.inf, le)
        v2 = jnp.max(le2, axis=-1, keepdims=True)
        i2 = jnp.min(jnp.where(le2 == v2, lanef, float(LANES)), axis=-1, keepdims=True)
        e2 = jnp.exp(v2 - v1)
        w1 = p_grp * (1.0 / (1.0 + e2))
        w2 = p_grp * (e2 / (1.0 + e2))
        gate_s[...] = jnp.where(lanef == i1, w1, jnp.where(lanef == i2, w2, 0.0))
        acc_s[...] = jnp.zeros(acc_s.shape, F32)

    hb = h_s[...]
    ge = jnp.sum(jnp.where(lane == N_GROUPS + e, gate_s[...], 0.0), axis=-1, keepdims=True)
    u = _mm(hb, w1_ref[...])
    hid = (u * jax.nn.sigmoid(u)) * _mm(hb, w3_ref[...]) * ge
    acc_s[...] += _mm(hid.astype(BF16), w2_ref[...])

    @pl.when(e == pl.num_programs(2) - 1)
    def _():
        o_ref[...] = x_ref[...] + gt_ref[...] * acc_s[...]


def _moe_call(x, sc, sh, gt, g_norm, wr, br, w1, w3, w2, layer, *, tm):
    bsz, t, _ = x.shape
    mrows = sc.shape[1]
    mblk = 1 if mrows == 1 else tm
    mod_map = (lambda b, i, e: (b, 0, 0)) if mrows == 1 else (lambda b, i, e: (b, i, 0))
    tok = pl.BlockSpec((None, tm, D_MODEL), lambda b, i, e: (b, i, 0))
    mod = pl.BlockSpec((None, mblk, D_MODEL), mod_map)
    return pl.pallas_call(
        _moe_body,
        out_shape=jax.ShapeDtypeStruct(x.shape, F32),
        grid=(bsz, t // tm, N_EXPERTS),
        in_specs=[tok, mod, mod, mod,
                  pl.BlockSpec((None, None, 1, D_MODEL), lambda b, i, e: (layer, 1, 0, 0)),
                  pl.BlockSpec((None, D_MODEL, LANES), lambda b, i, e: (layer, 0, 0)),
                  pl.BlockSpec((None, 1, LANES), lambda b, i, e: (layer, 0, 0)),
                  pl.BlockSpec((None, None, D_MODEL, D_EXPERT), lambda b, i, e: (layer, e, 0, 0)),
                  pl.BlockSpec((None, None, D_MODEL, D_EXPERT), lambda b, i, e: (layer, e, 0, 0)),
                  pl.BlockSpec((None, None, D_EXPERT, D_MODEL), lambda b, i, e: (layer, e, 0, 0))],
        out_specs=tok,
        scratch_shapes=[pltpu.VMEM((tm, D_MODEL), BF16), pltpu.VMEM((tm, LANES), F32),
                        pltpu.VMEM((tm, D_MODEL), F32)],
        compiler_params=_cparams(("parallel", "parallel", "arbitrary")),
        name="moe",
    )(x, sc, sh, gt, g_norm, wr, br, w1, w3, w2)


def _rope_tables(pos):
    inv_freq = ROPE_THETA ** (-jnp.arange(0, HEAD_DIM, 2, dtype=F32) / HEAD_DIM)
    ang = pos.astype(F32)[:, None] * inv_freq[None, :]
    cos = jnp.cos(ang)
    sin = jnp.sin(ang)
    cos_t = jnp.concatenate([cos, cos, cos, cos], axis=1)
    sin_t = jnp.concatenate([-sin, sin, -sin, sin], axis=1)
    return cos_t, sin_t


def _even_odd(x):
    return jnp.concatenate([x[:, 0::2], x[:, 1::2]], axis=1)


def _tile(n, pref):
    return pref if n % pref == 0 else n


def _forward(x_prompt, x_sample, cache_nsa, cache_sb, state_win, page_table, c_prompt, c_sample,
             w_ada, b_ada, g_norm, w_in, w_qk_norm, w_cmp, w_br_a, w_br_b, w_out,
             w_route_group, b_route_group, w_route_expert, b_route_expert,
             w_expert_gate, w_expert_up, w_expert_down):
    depth = w_in.shape[0]
    bp, seq, _ = x_prompt.shape
    nseq, tn, _ = x_sample.shape
    n_phys, page = cache_nsa.shape[1], cache_nsa.shape[2]
    past_len = page_table.shape[1] * page

    pad_gn = jnp.zeros((depth, D_MODEL, LANES - 3 * N_HEADS_NSA), F32)
    w_in_p = jnp.concatenate(
        [w_in[:, :, _QPERM], w_in[:, :, 512:1280], w_in[:, :, 1280:1304], pad_gn, w_in[:, :, 1304:]],
        axis=-1).astype(BF16)
    qkn_t = jnp.concatenate([w_qk_norm, w_qk_norm], axis=-1)
    wc_t = jnp.concatenate([w_cmp, w_cmp], axis=-1)
    wa_p = w_br_a[:, _QPERM, :].astype(BF16)
    wb_p = w_br_b.astype(BF16)
    wo_p = w_out.astype(BF16)
    pad_r = jnp.zeros((depth, D_MODEL, LANES - N_GROUPS - N_EXPERTS), F32)
    wr_p = jnp.concatenate([w_route_group, w_route_expert, pad_r], axis=-1).astype(BF16)
    br_p = jnp.concatenate([b_route_group, b_route_expert,
                            jnp.zeros((depth, LANES - N_GROUPS - N_EXPERTS), F32)], axis=-1)
    br_p = br_p.reshape(depth, 1, LANES)
    w1_p = w_expert_gate.astype(BF16)
    w3_p = w_expert_up.astype(BF16)
    w2_p = w_expert_down.astype(BF16)
    g_norm4 = g_norm.reshape(depth, 2, 1, D_MODEL)
    r = lax.broadcasted_iota(jnp.int32, (2 * LANES, LANES), 0)
    c = lax.broadcasted_iota(jnp.int32, (2 * LANES, LANES), 1)
    bd = jnp.where((r % LANES) // HEAD_DIM == c // HEAD_DIM, 1.0, 0.0).astype(BF16)

    cos_p, sin_p = _rope_tables(jnp.arange(seq))
    cos_s, sin_s = _rope_tables(past_len + (jnp.arange(nseq * tn) % tn))

    cache_nsa_v = cache_nsa.reshape(depth, n_phys, page, 4 * N_KV_NSA * HEAD_DIM)
    cache_sb_v = cache_sb.reshape(depth, n_phys, page, 2 * N_HEADS_SB * HEAD_DIM)
    state_win_v = state_win.reshape(depth, nseq, state_win.shape[2], 2 * N_KV_NSA * HEAD_DIM)

    n_c = bp + nseq
    n_cp = -(-n_c // 8) * 8
    c_all = jnp.concatenate([c_prompt, c_sample, jnp.zeros((n_cp - n_c, D_MODEL), F32)], axis=0)
    mod = _ada_call(c_all, w_ada, b_ada)

    def mods(l):
        mp = mod[l, 0:bp].reshape(bp, 1, 6, D_MODEL)
        ms_ = jnp.broadcast_to(mod[l, bp:n_c].reshape(nseq, 1, 6, D_MODEL), (nseq, tn, 6, D_MODEL))
        ms_ = ms_.reshape(1, nseq * tn, 6, D_MODEL)
        return [mp[:, :, k] for k in range(6)], [ms_[:, :, k] for k in range(6)]

    xp = x_prompt
    xs = x_sample.reshape(1, nseq * tn, D_MODEL)
    tm_p = _tile(seq, 256)
    tm_s = _tile(nseq * tn, 256)
    outs = [[] for _ in range(6)]
    for l in range(depth):
        (sh1, sc1, gt1, sh2, sc2, gt2), (sh1s, sc1s, gt1s, sh2s, sc2s, gt2s) = mods(l)

        (q, nsa, nsab, win, winb, gn, sbq, sbkv, sbkvb, gm, kc, vc) = _inproj_call(
            xp, sc1, sh1, g_norm4, w_in_p, qkn_t, cos_p, sin_p, bd, wc_t, l, tm=tm_p, with_cmp=True,
            qdtype=BF16)
        o_n = _nsa_prompt_call(q, gn, _even_odd(kc).astype(BF16), _even_odd(vc).astype(BF16), nsab, winb)
        o_s = _sb_prompt_call(sbq, sbkvb)
        xp = _post_call(xp, gt1, o_n, o_s, gm, wa_p, wb_p, wo_p, l, tm=_tile(seq, 512))
        xp = _moe_call(xp, sc2, sh2, gt2, g_norm4, wr_p, br_p, w1_p, w3_p, w2_p, l, tm=_tile(seq, 1024))
        outs[0].append(nsa.reshape(bp, seq, 4, N_KV_NSA, HEAD_DIM))
        outs[1].append(sbkv.reshape(bp, seq, 2, N_HEADS_SB, HEAD_DIM))
        nw = min(WINDOW, seq)
        outs[2].append(win[:, seq - nw:].reshape(bp, nw, 2, N_KV_NSA, HEAD_DIM))

        (qs_, nsas, _, wins, _, gns, sbqs, sbkvs, _, gms) = _inproj_call(
            xs, sc1s, sh1s, g_norm4, w_in_p, qkn_t, cos_s, sin_s, bd, wc_t, l, tm=tm_s, with_cmp=False,
            qdtype=F32)
        sq = lambda a: a.reshape(nseq, tn, a.shape[-1])
        o_ns, win_new = _sample_nsa_call(
            page_table, sq(qs_), sq(gns), sq(nsas), sq(wins), wc_t, state_win_v, cache_nsa_v, l)

        n_short = min(SB_RECENT_PAGES, page_table.shape[1])
        o_short, low = _sample_sb_call(page_table, sq(sbqs), sq(sbkvs), cache_sb_v, l, n_short)

        def full_sweep(layer=l, sbqs=sbqs, sbkvs=sbkvs):
            return _sample_sb_call(page_table, sq(sbqs), sq(sbkvs), cache_sb_v, layer, page_table.shape[1])[0]

        o_ss = lax.cond(jnp.min(low) >= SB_UNDERFLOW, lambda: o_short, full_sweep)
        xs = _post_call(xs, gt1s, o_ns.reshape(1, nseq * tn, 512), o_ss.reshape(1, nseq * tn, 512), gms,
                        wa_p, wb_p, wo_p, l, tm=tm_s)
        xs = _moe_call(xs, sc2s, sh2s, gt2s, g_norm4, wr_p, br_p, w1_p, w3_p, w2_p, l, tm=tm_s)
        outs[3].append(nsas.reshape(nseq, tn, 4, N_KV_NSA, HEAD_DIM))
        outs[4].append(sbkvs.reshape(nseq, tn, 2, N_HEADS_SB, HEAD_DIM))
        outs[5].append(win_new.reshape(nseq, win_new.shape[1], 2, N_KV_NSA, HEAD_DIM))

    return (xp, xs.reshape(nseq, tn, D_MODEL)) + tuple(jnp.stack(o) for o in outs)


def kernel(x_prompt, x_sample, cache_nsa, cache_sb, state_win, page_table, c_prompt, c_sample, w_ada, b_ada, g_norm, w_in, w_qk_norm, w_cmp, w_br_a, w_br_b, w_out, w_route_group, b_route_group, w_route_expert, b_route_expert, w_expert_gate, w_expert_up, w_expert_down):
    return _forward(x_prompt, x_sample, cache_nsa, cache_sb, state_win, page_table, c_prompt, c_sample,
                    w_ada, b_ada, g_norm, w_in, w_qk_norm, w_cmp, w_br_a, w_br_b, w_out,
                    w_route_group, b_route_group, w_route_expert, b_route_expert,
                    w_expert_gate, w_expert_up, w_expert_down)
```
